```python
import jax, jax.numpy as jnp
from jax import lax
import numpy as np

D_MODEL = 4096
BATCH = 2
SEQ = 4096
DEPTH = 2

GRID_W = 64
CTX_LEN = 256
HEAD_DIM = 128
N_Q_HEADS = 16
N_KV_HEADS = 4
GROUP = N_Q_HEADS // N_KV_HEADS
ATTN_WIDTH = N_Q_HEADS * HEAD_DIM
KV_WIDTH = N_KV_HEADS * HEAD_DIM
CONV_CH = D_MODEL // 2
CONV_WIDTH = 31
CONV_PAD = CONV_WIDTH // 2
D_FF = 2 * D_MODEL
N_MOD = 9
ROPE_THETA = 10000.0
EPS = 1e-6
Q_BLOCK = 128
KV_COLS = 2 * KV_WIDTH
Q_END = KV_COLS + ATTN_WIDTH
GLU_END = Q_END + 2 * CONV_CH
IN_COLS = GLU_END + 2 * D_MODEL

kernel_name = 'hybrid_gqa_conformer_dit_block'


def rms_norm(x, g):
    xf = x.astype(jnp.float32)
    y = xf * lax.rsqrt(jnp.mean(xf * xf, axis=-1, keepdims=True) + EPS)
    return (y * g.astype(jnp.float32)).astype(x.dtype)


def modulate(x, g, shift, scale):
    return rms_norm(x, g) * (1.0 + scale) + shift


def swiglu(h, w_in, w_out):
    a, b = jnp.split(h @ w_in, 2, axis=-1)
    return (jax.nn.silu(a) * b) @ w_out


def grid_rope(rows):
    row_idx = jnp.repeat(jnp.arange(rows, dtype=jnp.float32), GRID_W)
    col_idx = jnp.tile(jnp.arange(GRID_W, dtype=jnp.float32), rows)
    axis_dim = HEAD_DIM // 2
    inv_freq = ROPE_THETA ** (-jnp.arange(0, axis_dim, 2, dtype=jnp.float32) / axis_dim)
    ang = jnp.concatenate([row_idx[:, None] * inv_freq, col_idx[:, None] * inv_freq], axis=-1)
    return jnp.cos(ang), jnp.sin(ang)


def apply_rope(x, cos, sin):
    half = HEAD_DIM // 2
    xf = x.astype(jnp.float32)
    x1, x2 = xf[..., :half], xf[..., half:]
    cs, sn = cos[None, :, None, :], sin[None, :, None, :]
    return jnp.concatenate([x1 * cs - x2 * sn, x2 * cs + x1 * sn], axis=-1).astype(x.dtype)


def heads(t, n):
    return t.reshape(*t.shape[:-1], n, HEAD_DIM)


def attend(q, k, v):
    s = jnp.einsum('bqkgd,btkd->bkgqt', q, k, preferred_element_type=jnp.float32) * (HEAD_DIM ** -0.5)
    p = jax.nn.softmax(s, axis=-1).astype(v.dtype)
    return jnp.einsum('bkgqt,btkd->bqkgd', p, v)


def latent_attention(q, k_all, v_all):
    b, s = q.shape[0], q.shape[1]
    nblk = s // Q_BLOCK
    qb = q.reshape(b, nblk, Q_BLOCK, N_KV_HEADS, GROUP, HEAD_DIM).swapaxes(0, 1)
    ob = lax.map(lambda qq: attend(qq, k_all, v_all), qb)
    return ob.swapaxes(0, 1).reshape(b, s, ATTN_WIDTH)


def conv_module(glu_in, w_dw, b_dw, g_norm, w_o):
    a, gt = jnp.split(glu_in, 2, axis=-1)
    u = a * jax.nn.sigmoid(gt)
    u = lax.conv_general_dilated(u, w_dw[:, None, :].astype(u.dtype), window_strides=(1,),
                                 padding=[(CONV_PAD, CONV_PAD)], dimension_numbers=('NWC', 'WIO', 'NWC'),
                                 feature_group_count=CONV_CH) + b_dw
    u = jax.nn.silu(rms_norm(u, g_norm))
    return u @ w_o


def mixer_merge(attn_o, conv_o, gates, w_out):
    g_attn, g_conv = jnp.split(jax.nn.sigmoid(gates), 2, axis=-1)
    return (g_attn * attn_o + g_conv * conv_o) @ w_out


def half_ffn(h, m, g, w_in, w_out, o):
    return h + 0.5 * m[:, :, o + 2] * swiglu(modulate(h, g, m[:, :, o], m[:, :, o + 1]), w_in, w_out)


def hybrid_layer(x, ctx, c, c_ctx, cos, sin, update_ctx, w_mod, b_mod, norm_ffn1, w_ffn1_in, w_ffn1_out,
                 norm_mix, w_in, q_norm, k_norm, w_attn_o, conv_dw, conv_b, conv_norm, w_conv_o, w_out,
                 norm_ffn2, w_ffn2_in, w_ffn2_out):
    b = x.shape[0]
    n_ctx = ctx.shape[1]
    m_x = (jax.nn.silu(c) @ w_mod + b_mod).reshape(b, 1, N_MOD, D_MODEL)
    m_c = (jax.nn.silu(c_ctx) @ w_mod + b_mod).reshape(1, 1, N_MOD, D_MODEL)

    x = half_ffn(x, m_x, norm_ffn1, w_ffn1_in, w_ffn1_out, 0)
    ctx = half_ffn(ctx, m_c, norm_ffn1, w_ffn1_in, w_ffn1_out, 0)

    hx = modulate(x, norm_mix, m_x[:, :, 3], m_x[:, :, 4])
    hc = modulate(ctx, norm_mix, m_c[:, :, 3], m_c[:, :, 4])
    px = hx @ w_in
    pc = hc @ (w_in if update_ctx else w_in[:, :KV_COLS])

    k_c = rms_norm(heads(pc[..., :KV_WIDTH], N_KV_HEADS), k_norm)
    v_c = heads(pc[..., KV_WIDTH:KV_COLS], N_KV_HEADS)
    k_x = apply_rope(rms_norm(heads(px[..., :KV_WIDTH], N_KV_HEADS), k_norm), cos, sin)
    v_x = heads(px[..., KV_WIDTH:KV_COLS], N_KV_HEADS)
    q_x = apply_rope(rms_norm(heads(px[..., KV_COLS:Q_END], N_Q_HEADS), q_norm), cos, sin)
    k_all = jnp.concatenate([k_c, k_x], axis=1)
    v_all = jnp.concatenate([v_c, v_x], axis=1)

    attn_x = latent_attention(q_x, k_all, v_all) @ w_attn_o
    conv_x = conv_module(px[..., Q_END:GLU_END], conv_dw, conv_b, conv_norm, w_conv_o)
    x = x + m_x[:, :, 5] * mixer_merge(attn_x, conv_x, px[..., GLU_END:], w_out)

    if update_ctx:
        q_c = rms_norm(heads(pc[..., KV_COLS:Q_END], N_Q_HEADS), q_norm)
        q_c = q_c.reshape(b, n_ctx, N_KV_HEADS, GROUP, HEAD_DIM)
        attn_c = attend(q_c, k_c, v_c).reshape(b, n_ctx, ATTN_WIDTH) @ w_attn_o
        conv_c = conv_module(pc[..., Q_END:GLU_END], conv_dw, conv_b, conv_norm, w_conv_o)
        ctx = ctx + m_c[:, :, 5] * mixer_merge(attn_c, conv_c, pc[..., GLU_END:], w_out)
        ctx = half_ffn(ctx, m_c, norm_ffn2, w_ffn2_in, w_ffn2_out, 6)

    x = half_ffn(x, m_x, norm_ffn2, w_ffn2_in, w_ffn2_out, 6)
    return x, ctx


def setup_inputs(seed: int = 0) -> dict:
    key = jax.random.key(seed)
    ks = jax.random.split(key, 24)
    f32 = jnp.float32

    def nrm(k, shape, fan_in):
        return jax.random.normal(k, shape, f32) * (fan_in ** -0.5)

    def gain(k, shape):
        return 1.0 + 0.05 * jax.random.normal(k, shape, f32)

    L = DEPTH
    return {
        'x': jax.random.normal(ks[0], (BATCH, SEQ, D_MODEL), f32),
        'c': jax.random.normal(ks[1], (BATCH, D_MODEL), f32),
        'ctx': jax.random.normal(ks[2], (BATCH, CTX_LEN, D_MODEL), f32),
        'c_ctx': jax.random.normal(ks[3], (D_MODEL,), f32),
        'w_mod': nrm(ks[4], (L, D_MODEL, N_MOD * D_MODEL), D_MODEL),
        'b_mod': 0.02 * jax.random.normal(ks[5], (L, N_MOD * D_MODEL), f32),
        'norm_ffn1': gain(ks[6], (L, D_MODEL)),
        'w_ffn1_in': nrm(ks[7], (L, D_MODEL, 2 * D_FF), D_MODEL),
        'w_ffn1_out': nrm(ks[8], (L, D_FF, D_MODEL), D_FF),
        'norm_mix': gain(ks[9], (L, D_MODEL)),
        'w_in': nrm(ks[10], (L, D_MODEL, IN_COLS), D_MODEL),
        'q_norm': gain(ks[11], (L, HEAD_DIM)),
        'k_norm': gain(ks[12], (L, HEAD_DIM)),
        'w_attn_o': nrm(ks[13], (L, ATTN_WIDTH, D_MODEL), ATTN_WIDTH),
        'conv_dw': nrm(ks[14], (L, CONV_WIDTH, CONV_CH), CONV_WIDTH),
        'conv_b': 0.02 * jax.random.normal(ks[15], (L, CONV_CH), f32),
        'conv_norm': gain(ks[16], (L, CONV_CH)),
        'w_conv_o': nrm(ks[17], (L, CONV_CH, D_MODEL), CONV_CH),
        'w_out': nrm(ks[18], (L, D_MODEL, D_MODEL), D_MODEL),
        'norm_ffn2': gain(ks[19], (L, D_MODEL)),
        'w_ffn2_in': nrm(ks[20], (L, D_MODEL, 2 * D_FF), D_MODEL),
        'w_ffn2_out': nrm(ks[21], (L, D_FF, D_MODEL), D_FF),
    }


def reference(x, c, ctx, c_ctx, w_mod, b_mod, norm_ffn1, w_ffn1_in, w_ffn1_out, norm_mix, w_in, q_norm,
              k_norm, w_attn_o, conv_dw, conv_b, conv_norm, w_conv_o, w_out, norm_ffn2, w_ffn2_in, w_ffn2_out):
    ROWS = x.shape[1] // GRID_W
    cos, sin = grid_rope(ROWS)
    for i in range(DEPTH):
        x, ctx = hybrid_layer(x, ctx, c, c_ctx, cos, sin, i < DEPTH - 1,
                              w_mod[i], b_mod[i], norm_ffn1[i], w_ffn1_in[i], w_ffn1_out[i],
                              norm_mix[i], w_in[i], q_norm[i], k_norm[i], w_attn_o[i],
                              conv_dw[i], conv_b[i], conv_norm[i], w_conv_o[i], w_out[i],
                              norm_ffn2[i], w_ffn2_in[i], w_ffn2_out[i])
    return x
```

```python
import functools

import jax
import jax.numpy as jnp
from jax import lax
from jax.experimental import pallas as pl
from jax.experimental.pallas import tpu as pltpu

D_MODEL = 4096
BATCH = 2
SEQ = 4096
DEPTH = 2
GRID_W = 64
CTX_LEN = 256
HEAD_DIM = 128
N_Q_HEADS = 16
N_KV_HEADS = 4
GROUP = N_Q_HEADS // N_KV_HEADS
ATTN_WIDTH = N_Q_HEADS * HEAD_DIM
KV_WIDTH = N_KV_HEADS * HEAD_DIM
CONV_CH = D_MODEL // 2
CONV_WIDTH = 31
CONV_PAD = CONV_WIDTH // 2
D_FF = 2 * D_MODEL
N_MOD = 9
ROPE_THETA = 10000.0
EPS = 1e-6
KV_COLS = 2 * KV_WIDTH
Q_END = KV_COLS + ATTN_WIDTH
GLU_END = Q_END + 2 * CONV_CH

M_LAT = BATCH * SEQ
M_CTX = BATCH * CTX_LEN
M_ALL = M_LAT + M_CTX
SEG_ROWS = 8

VMEM_LIMIT_BYTES = 56 * 1024 * 1024
CAST_ROWS = 512

F32 = jnp.float32
BF16 = jnp.bfloat16


def _params(n_axes):
    return pltpu.CompilerParams(dimension_semantics=("arbitrary",) * n_axes,
                                vmem_limit_bytes=VMEM_LIMIT_BYTES)


def _seg_select(p, rows):
    return jnp.where(rows < SEQ, p[0:1, :], jnp.where(rows < M_LAT, p[1:2, :], p[2:3, :]))


def _sigmoid(x):
    return 1.0 / (1.0 + jnp.exp(-x))


def _fused_matmul_kernel(*refs, n_x, w_x, n_e, n_o, bm, epilogue, x_prologue, use_scratch):
    n_w = len(w_x)
    x_refs = refs[:n_x]
    w_refs = refs[n_x:n_x + n_w]
    e_refs = refs[n_x + n_w:n_x + n_w + n_e]
    o_refs = refs[n_x + n_w + n_e:n_x + n_w + n_e + n_o]
    s_refs = refs[n_x + n_w + n_e + n_o:]
    i = pl.program_id(1)

    if use_scratch:
        @pl.when(i == 0)
        def _():
            for w_ref, s_ref in zip(w_refs, s_refs):
                def body(t, carry, w_ref=w_ref, s_ref=s_ref):
                    r = pl.ds(pl.multiple_of(t * CAST_ROWS, CAST_ROWS), CAST_ROWS)
                    s_ref[r, :] = w_ref[r, :].astype(BF16)
                    return carry
                lax.fori_loop(0, w_ref.shape[0] // CAST_ROWS, body, 0)
        w_vals = [s_ref[...] for s_ref in s_refs]
    else:
        w_vals = [w_ref[...].astype(BF16) for w_ref in w_refs]

    xs = [x_ref[...] for x_ref in x_refs]
    if x_prologue is not None:
        xs = [x_prologue(v).astype(BF16) for v in xs]
    accs = [jnp.dot(xs[xi], w, preferred_element_type=F32) for xi, w in zip(w_x, w_vals)]
    rows = i * bm + lax.broadcasted_iota(jnp.int32, (bm, 1), 0)
    outs = epilogue(accs, [e_ref[...] for e_ref in e_refs], rows)
    for o_ref, o in zip(o_refs, outs):
        o_ref[...] = o.astype(o_ref.dtype)


def _fused_matmul(name, xs, ws, extras, epilogue, outs, *, m_rows, bm, bn, n_tiles, x_prologue=None):
    assert m_rows % bm == 0
    n_m = m_rows // bm
    use_scratch = n_m > 1
    in_specs = [pl.BlockSpec((bm, xa.shape[1]), lambda j, i: (i, 0)) for xa in xs]
    for wa, layer, off, _ in ws:
        assert wa.shape[1] % CAST_ROWS == 0
        in_specs.append(pl.BlockSpec((None, wa.shape[1], bn),
                                     lambda j, i, layer=layer, off=off: (layer, 0, off + j)))
    for ea, kind, off in extras:
        if kind == "tile":
            in_specs.append(pl.BlockSpec((bm, bn), lambda j, i, off=off: (i, off + j)))
        elif kind == "rows":
            in_specs.append(pl.BlockSpec((bm, ea.shape[1]), lambda j, i: (i, 0)))
        elif kind == "seg":
            in_specs.append(pl.BlockSpec((SEG_ROWS, bn), lambda j, i, off=off: (0, off + j)))
        elif kind == "whole":
            in_specs.append(pl.BlockSpec(ea.shape, lambda j, i: (0, 0)))
        else:
            assert kind == "col"
            in_specs.append(pl.BlockSpec((1, bn), lambda j, i, off=off: (0, off + j)))
    out_specs = [pl.BlockSpec((bm, bn), lambda j, i: (i, j)) for _ in outs]
    out_shape = [jax.ShapeDtypeStruct((m_rows, n_tiles * bn), dt) for dt in outs]
    scratch = [pltpu.VMEM((wa.shape[1], bn), BF16) for wa, _, _, _ in ws] if use_scratch else []
    body = functools.partial(
        _fused_matmul_kernel, n_x=len(xs), w_x=tuple(w[3] for w in ws), n_e=len(extras), n_o=len(outs),
        bm=bm, epilogue=epilogue, x_prologue=x_prologue, use_scratch=use_scratch)
    res = pl.pallas_call(
        body, grid=(n_tiles, n_m), in_specs=in_specs, out_specs=out_specs, out_shape=out_shape,
        scratch_shapes=scratch, compiler_params=_params(2), name=name,
    )(*xs, *[w[0] for w in ws], *[e[0] for e in extras])
    return res


def _row_block(m_rows):
    return m_rows // 8


def _ep_bias(accs, extras, rows):
    return [accs[0] + extras[0]]


def _ep_swiglu(accs, extras, rows):
    a, b = accs
    return [a * _sigmoid(a) * b]


def _ep_residual(accs, extras, rows, *, weight):
    resid, gate = extras
    return [resid + (weight * _seg_select(gate, rows)) * accs[0]]


def _ep_plain(accs, extras, rows):
    return [accs[0]]


def _ep_head_norm_rope(accs, extras, rows, *, scale):
    g, cosf, sinf = extras
    acc = accs[0]
    chunks = []
    for h in range(acc.shape[1] // HEAD_DIM):
        t = acc[:, h * HEAD_DIM:(h + 1) * HEAD_DIM]
        t = t * lax.rsqrt(jnp.mean(t * t, axis=-1, keepdims=True) + EPS) * g
        t = t * cosf + pltpu.roll(t, HEAD_DIM // 2, 1) * sinf
        chunks.append(t * scale if scale != 1.0 else t)
    return [jnp.concatenate(chunks, axis=1)]


def _ep_glu(accs, extras, rows):
    a, gt = accs
    return [a * _sigmoid(gt)]


def _ep_sigmoid(accs, extras, rows):
    return [_sigmoid(accs[0])]


def _ep_merge(accs, extras, rows):
    g_attn, g_conv = extras
    return [g_attn.astype(F32) * accs[0] + g_conv.astype(F32) * accs[1]]


def _silu(x):
    return x * _sigmoid(x)


NORM_ROWS = 256


def _norm_mod_kernel(h_ref, g_ref, shift_ref, scale_ref, o_ref):
    rows = pl.program_id(0) * NORM_ROWS + lax.broadcasted_iota(jnp.int32, (NORM_ROWS, 1), 0)
    x = h_ref[...]
    y = x * lax.rsqrt(jnp.mean(x * x, axis=-1, keepdims=True) + EPS) * g_ref[...]
    o_ref[...] = (y * (1.0 + _seg_select(scale_ref[...], rows)) + _seg_select(shift_ref[...], rows)).astype(BF16)


def _norm_mod(name, h, g, m_all, o, m_rows):
    return pl.pallas_call(
        _norm_mod_kernel, grid=(m_rows // NORM_ROWS,),
        in_specs=[pl.BlockSpec((NORM_ROWS, D_MODEL), lambda i: (i, 0)),
                  pl.BlockSpec((1, D_MODEL), lambda i: (0, 0)),
                  pl.BlockSpec((SEG_ROWS, D_MODEL), lambda i: (0, o)),
                  pl.BlockSpec((SEG_ROWS, D_MODEL), lambda i: (0, o + 1))],
        out_specs=pl.BlockSpec((NORM_ROWS, D_MODEL), lambda i: (i, 0)),
        out_shape=jax.ShapeDtypeStruct((m_rows, D_MODEL), BF16),
        compiler_params=_params(1), name=name,
    )(h, g.reshape(1, D_MODEL), m_all, m_all)


ATTN_Q_ROWS = 256
ATTN_KV_ROWS = 512
LAT_Q_BLOCKS = SEQ // ATTN_Q_ROWS


def _attn_kernel(q_ref, kl_ref, vl_ref, kc_ref, vc_ref, o_ref, m_sc, l_sc, acc_sc):
    qi = pl.program_id(2)
    q = jnp.concatenate([q_ref[:, g * HEAD_DIM:(g + 1) * HEAD_DIM] for g in range(GROUP)], axis=0)
    m_sc[...] = jnp.full(m_sc.shape, -jnp.inf, F32)
    l_sc[...] = jnp.zeros(l_sc.shape, F32)
    acc_sc[...] = jnp.zeros(acc_sc.shape, F32)

    def update(k, v):
        s = lax.dot_general(q, k, (((1,), (1,)), ((), ())), preferred_element_type=F32)
        m_prev = m_sc[...]
        m_new = jnp.maximum(m_prev, jnp.max(s, axis=-1, keepdims=True))
        p = jnp.exp(s - m_new)
        alpha = jnp.exp(m_prev - m_new)
        l_sc[...] = alpha * l_sc[...] + jnp.sum(p, axis=-1, keepdims=True)
        acc_sc[...] = alpha * acc_sc[...] + jnp.dot(p.astype(BF16), v, preferred_element_type=F32)
        m_sc[...] = m_new

    @pl.when(qi < LAT_Q_BLOCKS)
    def _():
        def body(t, carry):
            r = pl.ds(pl.multiple_of(t * ATTN_KV_ROWS, ATTN_KV_ROWS), ATTN_KV_ROWS)
            update(kl_ref[r, :], vl_ref[r, :])
            return carry
        lax.fori_loop(0, SEQ // ATTN_KV_ROWS, body, 0)

    update(kc_ref[...], vc_ref[...])
    out = acc_sc[...] / l_sc[...]
    for g in range(GROUP):
        o_ref[:, g * HEAD_DIM:(g + 1) * HEAD_DIM] = out[g * ATTN_Q_ROWS:(g + 1) * ATTN_Q_ROWS].astype(BF16)


def _attention(name, q, k, v, with_ctx_queries):
    n_q = LAT_Q_BLOCKS + (1 if with_ctx_queries else 0)
    m_out = M_ALL if with_ctx_queries else M_LAT
    ctx_block0 = M_LAT // CTX_LEN
    assert ATTN_Q_ROWS == CTX_LEN

    def q_map(b, h, qi):
        return (jnp.where(qi < LAT_Q_BLOCKS, b * LAT_Q_BLOCKS + qi, ctx_block0 + b), h)

    lat_spec = pl.BlockSpec((SEQ, HEAD_DIM), lambda b, h, qi: (b, h))
    ctx_spec = pl.BlockSpec((CTX_LEN, HEAD_DIM), lambda b, h, qi: (ctx_block0 + b, h))
    stacked = GROUP * ATTN_Q_ROWS
    return pl.pallas_call(
        _attn_kernel, grid=(BATCH, N_KV_HEADS, n_q),
        in_specs=[pl.BlockSpec((ATTN_Q_ROWS, GROUP * HEAD_DIM), q_map), lat_spec, lat_spec, ctx_spec, ctx_spec],
        out_specs=pl.BlockSpec((ATTN_Q_ROWS, GROUP * HEAD_DIM), q_map),
        out_shape=jax.ShapeDtypeStruct((m_out, ATTN_WIDTH), BF16),
        scratch_shapes=[pltpu.VMEM((stacked, 1), F32), pltpu.VMEM((stacked, 1), F32),
                        pltpu.VMEM((stacked, HEAD_DIM), F32)],
        compiler_params=_params(3), name=name,
    )(q, k, v, k, v)


CONV_ROWS = 256
HALO = 16
LANES = 128


def _conv_kernel(prev_ref, cur_ref, next_ref, w_ref, b_ref, g_ref, o_ref, pad_sc, y_sc):
    i = pl.program_id(0)
    lat_blocks = SEQ // CONV_ROWS
    n_lat = BATCH * lat_blocks
    first = (i % lat_blocks == 0) | (i >= n_lat)
    last = (i % lat_blocks == lat_blocks - 1) | (i >= n_lat)
    pad_sc[0:HALO, :] = jnp.where(first, 0.0, prev_ref[...].astype(F32))
    pad_sc[HALO:HALO + CONV_ROWS, :] = cur_ref[...].astype(F32)
    pad_sc[HALO + CONV_ROWS:, :] = jnp.where(last, 0.0, next_ref[...].astype(F32))

    def chunk(c, ss):
        col = pl.ds(pl.multiple_of(c * LANES, LANES), LANES)
        acc = jnp.broadcast_to(b_ref[:, col], (CONV_ROWS, LANES))
        for k in range(CONV_WIDTH):
            start = HALO - CONV_PAD + k
            acc = acc + pad_sc[start:start + CONV_ROWS, col] * w_ref[k:k + 1, col]
        y_sc[:, col] = acc
        return ss + acc * acc

    ss = lax.fori_loop(0, CONV_CH // LANES, chunk, jnp.zeros((CONV_ROWS, LANES), F32))
    inv = lax.rsqrt(jnp.sum(ss, axis=-1, keepdims=True) * (1.0 / CONV_CH) + EPS)
    y = y_sc[...] * inv * g_ref[...]
    o_ref[...] = _silu(y).astype(BF16)


def _conv_module(name, u, w_dw, b_dw, g_norm, m_rows):
    assert CTX_LEN == CONV_ROWS
    per = CONV_ROWS // HALO
    n_halo = u.shape[0] // HALO
    return pl.pallas_call(
        _conv_kernel, grid=(m_rows // CONV_ROWS,),
        in_specs=[pl.BlockSpec((HALO, CONV_CH), lambda i: (jnp.maximum(i * per - 1, 0), 0)),
                  pl.BlockSpec((CONV_ROWS, CONV_CH), lambda i: (i, 0)),
                  pl.BlockSpec((HALO, CONV_CH), lambda i: (jnp.minimum((i + 1) * per, n_halo - 1), 0)),
                  pl.BlockSpec((CONV_WIDTH, CONV_CH), lambda i: (0, 0)),
                  pl.BlockSpec((1, CONV_CH), lambda i: (0, 0)),
                  pl.BlockSpec((1, CONV_CH), lambda i: (0, 0))],
        out_specs=pl.BlockSpec((CONV_ROWS, CONV_CH), lambda i: (i, 0)),
        out_shape=jax.ShapeDtypeStruct((m_rows, CONV_CH), BF16),
        scratch_shapes=[pltpu.VMEM((CONV_ROWS + 2 * HALO, CONV_CH), F32), pltpu.VMEM((CONV_ROWS, CONV_CH), F32)],
        compiler_params=_params(1), name=name,
    )(u, u, u, w_dw, b_dw.reshape(1, CONV_CH), g_norm.reshape(1, CONV_CH))


def _rope_tables():
    rows = SEQ // GRID_W
    row_idx = jnp.repeat(jnp.arange(rows, dtype=F32), GRID_W)
    col_idx = jnp.tile(jnp.arange(GRID_W, dtype=F32), rows)
    axis_dim = HEAD_DIM // 2
    inv_freq = ROPE_THETA ** (-jnp.arange(0, axis_dim, 2, dtype=F32) / axis_dim)
    ang = jnp.concatenate([row_idx[:, None] * inv_freq, col_idx[:, None] * inv_freq], axis=-1)
    cos, sin = jnp.cos(ang), jnp.sin(ang)
    cosf = jnp.concatenate([cos, cos], axis=-1)
    sinf = jnp.concatenate([-sin, sin], axis=-1)
    cosf = jnp.concatenate([cosf, cosf, jnp.ones((M_CTX, HEAD_DIM), F32)], axis=0)
    sinf = jnp.concatenate([sinf, sinf, jnp.zeros((M_CTX, HEAD_DIM), F32)], axis=0)
    return cosf, sinf


def _half_ffn(tag, h, m_all, g, w_in, w_out, layer, o, m_rows):
    hn = _norm_mod(f"{tag}_norm", h, g, m_all, o, m_rows)
    bm = _row_block(m_rows)
    bn = 256
    (u,) = _fused_matmul(f"{tag}_in", [hn], [(w_in, layer, 0, 0), (w_in, layer, D_FF // bn, 0)], [],
                         _ep_swiglu, [BF16], m_rows=m_rows, bm=bm, bn=bn, n_tiles=D_FF // bn)
    (h_new,) = _fused_matmul(
        f"{tag}_out", [u], [(w_out, layer, 0, 0)],
        [(h, "tile", 0), (m_all, "seg", (o + 2) * D_MODEL // bn)],
        functools.partial(_ep_residual, weight=0.5), [F32],
        m_rows=m_rows, bm=bm // 2, bn=bn, n_tiles=D_MODEL // bn)
    return h_new


def kernel(x, c, ctx, c_ctx, w_mod, b_mod, norm_ffn1, w_ffn1_in, w_ffn1_out, norm_mix, w_in, q_norm, k_norm,
           w_attn_o, conv_dw, conv_b, conv_norm, w_conv_o, w_out, norm_ffn2, w_ffn2_in, w_ffn2_out):
    h = jnp.concatenate([x.reshape(M_LAT, D_MODEL), ctx.reshape(M_CTX, D_MODEL)], axis=0)
    cvec = jnp.concatenate([c, c_ctx[None, :], jnp.zeros((SEG_ROWS - BATCH - 1, D_MODEL), F32)], axis=0)
    cosf, sinf = _rope_tables()
    bn = 512

    for layer in range(DEPTH):
        tag = f"l{layer}"
        update_ctx = layer < DEPTH - 1
        (m_all,) = _fused_matmul(
            f"{tag}_mod", [cvec], [(w_mod, layer, 0, 0)], [(b_mod[layer].reshape(1, -1), "col", 0)],
            _ep_bias, [F32], m_rows=SEG_ROWS, bm=SEG_ROWS, bn=bn, n_tiles=N_MOD * D_MODEL // bn, x_prologue=_silu)

        h = _half_ffn(f"{tag}_ffn1", h, m_all, norm_ffn1[layer], w_ffn1_in, w_ffn1_out, layer, 0, M_ALL)

        hn = _norm_mod(f"{tag}_mix_norm", h, norm_mix[layer], m_all, 3, M_ALL)
        m_mix = M_ALL if update_ctx else M_LAT
        rope = [(cosf, "rows", 0), (sinf, "rows", 0)]
        (k,) = _fused_matmul(
            f"{tag}_k", [hn], [(w_in, layer, 0, 0)], [(k_norm[layer].reshape(1, HEAD_DIM), "whole", 0)] + rope,
            functools.partial(_ep_head_norm_rope, scale=1.0), [BF16],
            m_rows=M_ALL, bm=_row_block(M_ALL), bn=bn, n_tiles=KV_WIDTH // bn)
        (v,) = _fused_matmul(
            f"{tag}_v", [hn], [(w_in, layer, KV_WIDTH // bn, 0)], [], _ep_plain, [BF16],
            m_rows=M_ALL, bm=_row_block(M_ALL), bn=bn, n_tiles=KV_WIDTH // bn)
        bm = _row_block(m_mix)
        (q,) = _fused_matmul(
            f"{tag}_q", [hn], [(w_in, layer, KV_COLS // bn, 0)],
            [(q_norm[layer].reshape(1, HEAD_DIM), "whole", 0)] + rope,
            functools.partial(_ep_head_norm_rope, scale=HEAD_DIM ** -0.5), [BF16],
            m_rows=m_mix, bm=bm, bn=bn, n_tiles=ATTN_WIDTH // bn)
        bg = 256
        (u,) = _fused_matmul(
            f"{tag}_glu", [hn], [(w_in, layer, Q_END // bg, 0), (w_in, layer, (Q_END + CONV_CH) // bg, 0)], [],
            _ep_glu, [BF16], m_rows=m_mix, bm=bm, bn=bg, n_tiles=CONV_CH // bg)
        (gates,) = _fused_matmul(
            f"{tag}_gates", [hn], [(w_in, layer, GLU_END // bn, 0)], [], _ep_sigmoid, [BF16],
            m_rows=m_mix, bm=bm, bn=bn, n_tiles=2 * D_MODEL // bn)

        attn = _attention(f"{tag}_attn", q, k, v, update_ctx)
        cv = _conv_module(f"{tag}_conv", u, conv_dw[layer], conv_b[layer], conv_norm[layer], m_mix)
        (merged,) = _fused_matmul(
            f"{tag}_merge", [attn, cv], [(w_attn_o, layer, 0, 0), (w_conv_o, layer, 0, 1)],
            [(gates, "tile", 0), (gates, "tile", D_MODEL // bn)], _ep_merge, [BF16],
            m_rows=m_mix, bm=bm, bn=bn, n_tiles=D_MODEL // bn)
        (h,) = _fused_matmul(
            f"{tag}_out", [merged], [(w_out, layer, 0, 0)],
            [(h, "tile", 0), (m_all, "seg", 5 * D_MODEL // bn)],
            functools.partial(_ep_residual, weight=1.0), [F32],
            m_rows=m_mix, bm=bm, bn=bn, n_tiles=D_MODEL // bn)

        h = _half_ffn(f"{tag}_ffn2", h, m_all, norm_ffn2[layer], w_ffn2_in, w_ffn2_out, layer, 6, m_mix)

    return h.reshape(BATCH, SEQ, D_MODEL)
```

```python
import functools

import jax
import jax.numpy as jnp
from jax import lax
from jax.experimental import pallas as pl
from jax.experimental.pallas import tpu as pltpu

D_MODEL = 4096
BATCH = 2
SEQ = 4096
DEPTH = 2
GRID_W = 64
CTX_LEN = 256
HEAD_DIM = 128
N_Q_HEADS = 16
N_KV_HEADS = 4
GROUP = N_Q_HEADS // N_KV_HEADS
ATTN_WIDTH = N_Q_HEADS * HEAD_DIM
KV_WIDTH = N_KV_HEADS * HEAD_DIM
CONV_CH = D_MODEL // 2
CONV_WIDTH = 31
CONV_PAD = CONV_WIDTH // 2
D_FF = 2 * D_MODEL
N_MOD = 9
ROPE_THETA = 10000.0
EPS = 1e-6
KV_COLS = 2 * KV_WIDTH
Q_END = KV_COLS + ATTN_WIDTH
GLU_END = Q_END + 2 * CONV_CH

M_LAT = BATCH * SEQ
M_CTX = BATCH * CTX_LEN
M_ALL = M_LAT + M_CTX
SEG_ROWS = 8

VMEM_LIMIT_BYTES = 56 * 1024 * 1024
CAST_ROWS = 512

F32 = jnp.float32
BF16 = jnp.bfloat16


def _params(n_axes):
    return pltpu.CompilerParams(dimension_semantics=("arbitrary",) * n_axes,
                                vmem_limit_bytes=VMEM_LIMIT_BYTES)


def _seg_select(p, rows):
    return jnp.where(rows < SEQ, p[0:1, :], jnp.where(rows < M_LAT, p[1:2, :], p[2:3, :]))


def _sigmoid(x):
    return 1.0 / (1.0 + jnp.exp(-x))


def _fused_matmul_kernel(*refs, n_x, w_x, e_kinds, n_o, bm, row_split, epilogue, x_prologue, use_scratch):
    n_w = len(w_x)
    n_e = len(e_kinds)
    x_refs = refs[:n_x]
    w_refs = refs[n_x:n_x + n_w]
    e_refs = refs[n_x + n_w:n_x + n_w + n_e]
    o_refs = refs[n_x + n_w + n_e:n_x + n_w + n_e + n_o]
    s_refs = refs[n_x + n_w + n_e + n_o:]
    i = pl.program_id(1)

    if use_scratch:
        @pl.when(i == 0)
        def _():
            for w_ref, s_ref in zip(w_refs, s_refs):
                def body(t, carry, w_ref=w_ref, s_ref=s_ref):
                    r = pl.ds(pl.multiple_of(t * CAST_ROWS, CAST_ROWS), CAST_ROWS)
                    s_ref[r, :] = w_ref[r, :].astype(BF16)
                    return carry
                lax.fori_loop(0, w_ref.shape[0] // CAST_ROWS, body, 0)
        w_vals = [s_ref[...] for s_ref in s_refs]
    else:
        w_vals = [w_ref[...].astype(BF16) for w_ref in w_refs]

    sub = bm // row_split
    for r in range(row_split):
        rs = slice(r * sub, (r + 1) * sub)
        xs = [x_ref[rs, :] for x_ref in x_refs]
        if x_prologue is not None:
            xs = [x_prologue(v).astype(BF16) for v in xs]
        accs = [jnp.dot(xs[xi], w, preferred_element_type=F32) for xi, w in zip(w_x, w_vals)]
        rows = i * bm + r * sub + lax.broadcasted_iota(jnp.int32, (sub, 1), 0)
        extras = [e_ref[rs, :] if kind in ("tile", "rows") else e_ref[...] for e_ref, kind in zip(e_refs, e_kinds)]
        outs = epilogue(accs, extras, rows)
        for o_ref, o in zip(o_refs, outs):
            o_ref[rs, :] = o.astype(o_ref.dtype)


def _fused_matmul(name, xs, ws, extras, epilogue, outs, *, m_rows, bm, bn, n_tiles, x_prologue=None,
                  k_width=None, k_block=0, row_split=2):
    assert m_rows % bm == 0
    n_m = m_rows // bm
    use_scratch = n_m > 1
    k_of = lambda a: a.shape[-1 if a.ndim == 2 else 1] if k_width is None else k_width
    in_specs = [pl.BlockSpec((bm, k_of(xa)), lambda j, i: (i, k_block)) for xa in xs]
    for wa, layer, off, _ in ws:
        assert k_of(wa) % CAST_ROWS == 0
        in_specs.append(pl.BlockSpec((None, k_of(wa), bn),
                                     lambda j, i, layer=layer, off=off: (layer, k_block, off + j)))
    for ea, kind, off in extras:
        if kind == "tile":
            in_specs.append(pl.BlockSpec((bm, bn), lambda j, i, off=off: (i, off + j)))
        elif kind == "rows":
            in_specs.append(pl.BlockSpec((bm, ea.shape[1]), lambda j, i: (i, 0)))
        elif kind == "seg":
            in_specs.append(pl.BlockSpec((SEG_ROWS, bn), lambda j, i, off=off: (0, off + j)))
        elif kind == "whole":
            in_specs.append(pl.BlockSpec(ea.shape, lambda j, i: (0, 0)))
        else:
            assert kind == "col"
            in_specs.append(pl.BlockSpec((1, bn), lambda j, i, off=off: (0, off + j)))
    out_specs = [pl.BlockSpec((bm, bn), lambda j, i: (i, j)) for _ in outs]
    out_shape = [jax.ShapeDtypeStruct((m_rows, n_tiles * bn), dt) for dt in outs]
    scratch = [pltpu.VMEM((k_of(wa), bn), BF16) for wa, _, _, _ in ws] if use_scratch else []
    body = functools.partial(
        _fused_matmul_kernel, n_x=len(xs), w_x=tuple(w[3] for w in ws), e_kinds=tuple(e[1] for e in extras),
        n_o=len(outs), bm=bm, row_split=row_split, epilogue=epilogue, x_prologue=x_prologue,
        use_scratch=use_scratch)
    res = pl.pallas_call(
        body, grid=(n_tiles, n_m), in_specs=in_specs, out_specs=out_specs, out_shape=out_shape,
        scratch_shapes=scratch, compiler_params=_params(2), name=name,
    )(*xs, *[w[0] for w in ws], *[e[0] for e in extras])
    return res


def _row_block(m_rows):
    return m_rows // 8


def _ep_bias(accs, extras, rows):
    return [accs[0] + extras[0]]


def _ep_swiglu(accs, extras, rows):
    a, b = accs
    return [a * _sigmoid(a) * b]


def _ep_residual(accs, extras, rows, *, weight):
    resid, gate = extras[:2]
    acc = accs[0] if len(extras) == 2 else extras[2] + accs[0]
    return [resid + (weight * _seg_select(gate, rows)) * acc]


def _ep_plain(accs, extras, rows):
    return [accs[0]]


def _ep_head_norm_rope(accs, extras, rows, *, scale):
    g, cosf, sinf = extras
    acc = accs[0]
    chunks = []
    for h in range(acc.shape[1] // HEAD_DIM):
        t = acc[:, h * HEAD_DIM:(h + 1) * HEAD_DIM]
        t = t * lax.rsqrt(jnp.mean(t * t, axis=-1, keepdims=True) + EPS) * g
        t = t * cosf + pltpu.roll(t, HEAD_DIM // 2, 1) * sinf
        chunks.append(t * scale if scale != 1.0 else t)
    return [jnp.concatenate(chunks, axis=1)]


def _ep_glu(accs, extras, rows):
    a, gt = accs
    return [a * _sigmoid(gt)]


def _ep_sigmoid(accs, extras, rows):
    return [_sigmoid(accs[0])]


def _ep_merge(accs, extras, rows):
    g_attn, g_conv = extras
    return [g_attn.astype(F32) * accs[0] + g_conv.astype(F32) * accs[1]]


def _silu(x):
    return x * _sigmoid(x)


NORM_ROWS = 256


def _norm_mod_kernel(h_ref, g_ref, shift_ref, scale_ref, o_ref):
    seg = pl.ds(jnp.minimum(pl.program_id(0) // (SEQ // NORM_ROWS), BATCH), 1)
    x = h_ref[...]
    inv = lax.rsqrt(jnp.mean(x * x, axis=-1, keepdims=True) + EPS)
    o_ref[...] = (x * inv * (g_ref[...] * (1.0 + scale_ref[seg, :])) + shift_ref[seg, :]).astype(BF16)


def _norm_mod(name, h, g, m_all, o, m_rows):
    return pl.pallas_call(
        _norm_mod_kernel, grid=(m_rows // NORM_ROWS,),
        in_specs=[pl.BlockSpec((NORM_ROWS, D_MODEL), lambda i: (i, 0)),
                  pl.BlockSpec((1, D_MODEL), lambda i: (0, 0)),
                  pl.BlockSpec((SEG_ROWS, D_MODEL), lambda i: (0, o)),
                  pl.BlockSpec((SEG_ROWS, D_MODEL), lambda i: (0, o + 1))],
        out_specs=pl.BlockSpec((NORM_ROWS, D_MODEL), lambda i: (i, 0)),
        out_shape=jax.ShapeDtypeStruct((m_rows, D_MODEL), BF16),
        compiler_params=_params(1), name=name,
    )(h, g.reshape(1, D_MODEL), m_all, m_all)


ATTN_Q_ROWS = 256
ATTN_KV_ROWS = 512
LAT_Q_BLOCKS = SEQ // ATTN_Q_ROWS


def _attn_kernel(q_ref, kl_ref, vl_ref, kc_ref, vc_ref, o_ref, m_sc, l_sc, acc_sc):
    qi = pl.program_id(2)
    m_sc[...] = jnp.full(m_sc.shape, -jnp.inf, F32)
    l_sc[...] = jnp.zeros(l_sc.shape, F32)
    acc_sc[...] = jnp.zeros(acc_sc.shape, F32)

    def scores(k):
        return tuple(
            lax.dot_general(q_ref[:, g * HEAD_DIM:(g + 1) * HEAD_DIM], k, (((1,), (1,)), ((), ())),
                            preferred_element_type=F32)
            for g in range(GROUP))

    def update(scores_kv, v):
        for g, s in enumerate(scores_kv):
            chunks = [s[:, c * LANES:(c + 1) * LANES] for c in range(s.shape[1] // LANES)]
            m_prev = m_sc[g]
            m_new = jnp.maximum(m_prev, jnp.max(functools.reduce(jnp.maximum, chunks), axis=-1, keepdims=True))
            alpha = jnp.exp(m_prev - m_new)
            ps = [jnp.exp(c - m_new) for c in chunks]
            l_sc[g] = alpha * l_sc[g] + functools.reduce(jnp.add, ps)
            p = jnp.concatenate([x.astype(BF16) for x in ps], axis=1)
            acc_sc[g] = alpha * acc_sc[g] + jnp.dot(p, v, preferred_element_type=F32)
            m_sc[g] = m_new

    def lat_rows(t):
        return pl.ds(pl.multiple_of(t * ATTN_KV_ROWS, ATTN_KV_ROWS), ATTN_KV_ROWS)

    @pl.when(qi < LAT_Q_BLOCKS)
    def _():
        def body(t, s_cur):
            s_next = scores(kl_ref[lat_rows(t + 1), :])
            update(s_cur, vl_ref[lat_rows(t), :])
            return s_next
        n_lat = SEQ // ATTN_KV_ROWS
        s_last = lax.fori_loop(0, n_lat - 1, body, scores(kl_ref[lat_rows(0), :]))
        update(s_last, vl_ref[lat_rows(n_lat - 1), :])

    update(scores(kc_ref[...]), vc_ref[...])
    for g in range(GROUP):
        l = jnp.sum(l_sc[g], axis=-1, keepdims=True)
        o_ref[:, g * HEAD_DIM:(g + 1) * HEAD_DIM] = (acc_sc[g] / l).astype(BF16)


def _attention(name, q, k, v, with_ctx_queries):
    n_q = LAT_Q_BLOCKS + (1 if with_ctx_queries else 0)
    m_out = M_ALL if with_ctx_queries else M_LAT
    ctx_block0 = M_LAT // CTX_LEN
    assert ATTN_Q_ROWS == CTX_LEN

    def q_map(b, h, qi):
        return (jnp.where(qi < LAT_Q_BLOCKS, b * LAT_Q_BLOCKS + qi, ctx_block0 + b), h)

    lat_spec = pl.BlockSpec((SEQ, HEAD_DIM), lambda b, h, qi: (b, h))
    ctx_spec = pl.BlockSpec((CTX_LEN, HEAD_DIM), lambda b, h, qi: (ctx_block0 + b, h))
    stats = pltpu.VMEM((GROUP, ATTN_Q_ROWS, HEAD_DIM), F32)
    return pl.pallas_call(
        _attn_kernel, grid=(BATCH, N_KV_HEADS, n_q),
        in_specs=[pl.BlockSpec((ATTN_Q_ROWS, GROUP * HEAD_DIM), q_map), lat_spec, lat_spec, ctx_spec, ctx_spec],
        out_specs=pl.BlockSpec((ATTN_Q_ROWS, GROUP * HEAD_DIM), q_map),
        out_shape=jax.ShapeDtypeStruct((m_out, ATTN_WIDTH), BF16),
        scratch_shapes=[stats, stats, stats],
        compiler_params=_params(3), name=name,
    )(q, k, v, k, v)


CONV_ROWS = 256
HALO = 16
LANES = 128
SUBLANES = 8


def _conv_kernel(prev_ref, cur_ref, next_ref, w_ref, b_ref, g_ref, o_ref, pad_sc, y_sc):
    i = pl.program_id(0)
    lat_blocks = SEQ // CONV_ROWS
    n_lat = BATCH * lat_blocks
    first = (i % lat_blocks == 0) | (i >= n_lat)
    last = (i % lat_blocks == lat_blocks - 1) | (i >= n_lat)
    pad_sc[0:HALO, :] = jnp.where(first, 0.0, prev_ref[...].astype(F32))
    pad_sc[HALO:HALO + CONV_ROWS, :] = cur_ref[...].astype(F32)
    pad_sc[HALO + CONV_ROWS:, :] = jnp.where(last, 0.0, next_ref[...].astype(F32))

    def lanes(c):
        return pl.ds(pl.multiple_of(c * LANES, LANES), LANES)

    def chunk(c, ss):
        col = lanes(c)
        acc = jnp.broadcast_to(b_ref[:, col], (CONV_ROWS, LANES))
        for r in range(SUBLANES):
            part = None
            for k in range(CONV_WIDTH):
                start = HALO - CONV_PAD + k
                if start % SUBLANES != r:
                    continue
                base = start - r
                term = pad_sc[base:base + CONV_ROWS + SUBLANES, col] * w_ref[k:k + 1, col]
                part = term if part is None else part + term
            acc = acc + part[r:r + CONV_ROWS, :]
        y_sc[:, col] = acc
        return ss + acc * acc

    ss = lax.fori_loop(0, CONV_CH // LANES, chunk, jnp.zeros((CONV_ROWS, LANES), F32))
    inv = jnp.broadcast_to(lax.rsqrt(jnp.sum(ss, axis=-1, keepdims=True) * (1.0 / CONV_CH) + EPS),
                           (CONV_ROWS, LANES))

    def finish(c, carry):
        col = lanes(c)
        o_ref[:, col] = _silu(y_sc[:, col] * inv * g_ref[:, col]).astype(BF16)
        return carry

    lax.fori_loop(0, CONV_CH // LANES, finish, 0)


def _conv_module(name, u, w_dw, b_dw, g_norm, m_rows):
    assert CTX_LEN == CONV_ROWS
    per = CONV_ROWS // HALO
    n_halo = u.shape[0] // HALO
    return pl.pallas_call(
        _conv_kernel, grid=(m_rows // CONV_ROWS,),
        in_specs=[pl.BlockSpec((HALO, CONV_CH), lambda i: (jnp.maximum(i * per - 1, 0), 0)),
                  pl.BlockSpec((CONV_ROWS, CONV_CH), lambda i: (i, 0)),
                  pl.BlockSpec((HALO, CONV_CH), lambda i: (jnp.minimum((i + 1) * per, n_halo - 1), 0)),
                  pl.BlockSpec((CONV_WIDTH, CONV_CH), lambda i: (0, 0)),
                  pl.BlockSpec((1, CONV_CH), lambda i: (0, 0)),
                  pl.BlockSpec((1, CONV_CH), lambda i: (0, 0))],
        out_specs=pl.BlockSpec((CONV_ROWS, CONV_CH), lambda i: (i, 0)),
        out_shape=jax.ShapeDtypeStruct((m_rows, CONV_CH), BF16),
        scratch_shapes=[pltpu.VMEM((CONV_ROWS + 2 * HALO, CONV_CH), F32), pltpu.VMEM((CONV_ROWS, CONV_CH), F32)],
        compiler_params=_params(1), name=name,
    )(u, u, u, w_dw, b_dw.reshape(1, CONV_CH), g_norm.reshape(1, CONV_CH))


def _rope_tables():
    rows = SEQ // GRID_W
    row_idx = jnp.repeat(jnp.arange(rows, dtype=F32), GRID_W)
    col_idx = jnp.tile(jnp.arange(GRID_W, dtype=F32), rows)
    axis_dim = HEAD_DIM // 2
    inv_freq = ROPE_THETA ** (-jnp.arange(0, axis_dim, 2, dtype=F32) / axis_dim)
    ang = jnp.concatenate([row_idx[:, None] * inv_freq, col_idx[:, None] * inv_freq], axis=-1)
    cos, sin = jnp.cos(ang), jnp.sin(ang)
    cosf = jnp.concatenate([cos, cos], axis=-1)
    sinf = jnp.concatenate([-sin, sin], axis=-1)
    cosf = jnp.concatenate([cosf, cosf, jnp.ones((M_CTX, HEAD_DIM), F32)], axis=0)
    sinf = jnp.concatenate([sinf, sinf, jnp.zeros((M_CTX, HEAD_DIM), F32)], axis=0)
    return cosf, sinf


def _half_ffn(tag, h, m_all, g, w_in, w_out, layer, o, m_rows):
    hn = _norm_mod(f"{tag}_norm", h, g, m_all, o, m_rows)
    bm = _row_block(m_rows)
    bn = 256
    (u,) = _fused_matmul(f"{tag}_in", [hn], [(w_in, layer, 0, 0), (w_in, layer, D_FF // bn, 0)], [],
                         _ep_swiglu, [BF16], m_rows=m_rows, bm=bm, bn=bn, n_tiles=D_FF // bn)
    bn = 512
    half = D_FF // 2
    (part,) = _fused_matmul(f"{tag}_out_a", [u], [(w_out, layer, 0, 0)], [], _ep_plain, [F32],
                            m_rows=m_rows, bm=bm, bn=bn, n_tiles=D_MODEL // bn, k_width=half, k_block=0)
    (h_new,) = _fused_matmul(
        f"{tag}_out_b", [u], [(w_out, layer, 0, 0)],
        [(h, "tile", 0), (m_all, "seg", (o + 2) * D_MODEL // bn), (part, "tile", 0)],
        functools.partial(_ep_residual, weight=0.5), [F32],
        m_rows=m_rows, bm=bm, bn=bn, n_tiles=D_MODEL // bn, k_width=half, k_block=1)
    return h_new


def kernel(x, c, ctx, c_ctx, w_mod, b_mod, norm_ffn1, w_ffn1_in, w_ffn1_out, norm_mix, w_in, q_norm, k_norm,
           w_attn_o, conv_dw, conv_b, conv_norm, w_conv_o, w_out, norm_ffn2, w_ffn2_in, w_ffn2_out):
    h = jnp.concatenate([x.reshape(M_LAT, D_MODEL), ctx.reshape(M_CTX, D_MODEL)], axis=0)
    cvec = jnp.concatenate([c, c_ctx[None, :], jnp.zeros((SEG_ROWS - BATCH - 1, D_MODEL), F32)], axis=0)
    cosf, sinf = _rope_tables()
    bn = 512

    for layer in range(DEPTH):
        tag = f"l{layer}"
        update_ctx = layer < DEPTH - 1
        (m_all,) = _fused_matmul(
            f"{tag}_mod", [cvec], [(w_mod, layer, 0, 0)], [(b_mod[layer].reshape(1, -1), "col", 0)],
            _ep_bias, [F32], m_rows=SEG_ROWS, bm=SEG_ROWS, bn=bn, n_tiles=N_MOD * D_MODEL // bn, x_prologue=_silu,
            row_split=1)

        h = _half_ffn(f"{tag}_ffn1", h, m_all, norm_ffn1[layer], w_ffn1_in, w_ffn1_out, layer, 0, M_ALL)

        hn = _norm_mod(f"{tag}_mix_norm", h, norm_mix[layer], m_all, 3, M_ALL)
        m_mix = M_ALL if update_ctx else M_LAT
        rope = [(cosf, "rows", 0), (sinf, "rows", 0)]
        (k,) = _fused_matmul(
            f"{tag}_k", [hn], [(w_in, layer, 0, 0)], [(k_norm[layer].reshape(1, HEAD_DIM), "whole", 0)] + rope,
            functools.partial(_ep_head_norm_rope, scale=1.0), [BF16],
            m_rows=M_ALL, bm=_row_block(M_ALL), bn=bn, n_tiles=KV_WIDTH // bn, row_split=4)
        (v,) = _fused_matmul(
            f"{tag}_v", [hn], [(w_in, layer, KV_WIDTH // bn, 0)], [], _ep_plain, [BF16],
            m_rows=M_ALL, bm=_row_block(M_ALL), bn=bn, n_tiles=KV_WIDTH // bn)
        bm = _row_block(m_mix)
        (q,) = _fused_matmul(
            f"{tag}_q", [hn], [(w_in, layer, KV_COLS // bn, 0)],
            [(q_norm[layer].reshape(1, HEAD_DIM), "whole", 0)] + rope,
            functools.partial(_ep_head_norm_rope, scale=HEAD_DIM ** -0.5), [BF16],
            m_rows=m_mix, bm=bm, bn=bn, n_tiles=ATTN_WIDTH // bn, row_split=4)
        bg = 256
        (u,) = _fused_matmul(
            f"{tag}_glu", [hn], [(w_in, layer, Q_END // bg, 0), (w_in, layer, (Q_END + CONV_CH) // bg, 0)], [],
            _ep_glu, [BF16], m_rows=m_mix, bm=bm, bn=bg, n_tiles=CONV_CH // bg)
        (gates,) = _fused_matmul(
            f"{tag}_gates", [hn], [(w_in, layer, GLU_END // bn, 0)], [], _ep_sigmoid, [BF16],
            m_rows=m_mix, bm=bm, bn=bn, n_tiles=2 * D_MODEL // bn)

        attn = _attention(f"{tag}_attn", q, k, v, update_ctx)
        cv = _conv_module(f"{tag}_conv", u, conv_dw[layer], conv_b[layer], conv_norm[layer], m_mix)
        (merged,) = _fused_matmul(
            f"{tag}_merge", [attn, cv], [(w_attn_o, layer, 0, 0), (w_conv_o, layer, 0, 1)],
            [(gates, "tile", 0), (gates, "tile", D_MODEL // bn)], _ep_merge, [BF16],
            m_rows=m_mix, bm=bm, bn=bn, n_tiles=D_MODEL // bn)
        (h,) = _fused_matmul(
            f"{tag}_out", [merged], [(w_out, layer, 0, 0)],
            [(h, "tile", 0), (m_all, "seg", 5 * D_MODEL // bn)],
            functools.partial(_ep_residual, weight=1.0), [F32],
            m_rows=m_mix, bm=bm, bn=bn, n_tiles=D_MODEL // bn)

        h = _half_ffn(f"{tag}_ffn2", h, m_all, norm_ffn2[layer], w_ffn2_in, w_ffn2_out, layer, 6, m_mix)

    return h.reshape(BATCH, SEQ, D_MODEL)
```

```python
import functools

import jax
import jax.numpy as jnp
from jax import lax
from jax.experimental import pallas as pl
from jax.experimental.pallas import tpu as pltpu

D_MODEL = 4096
BATCH = 2
SEQ = 4096
DEPTH = 2
GRID_W = 64
CTX_LEN = 256
HEAD_DIM = 128
N_Q_HEADS = 16
N_KV_HEADS = 4
GROUP = N_Q_HEADS // N_KV_HEADS
ATTN_WIDTH = N_Q_HEADS * HEAD_DIM
KV_WIDTH = N_KV_HEADS * HEAD_DIM
CONV_CH = D_MODEL // 2
CONV_WIDTH = 31
CONV_PAD = CONV_WIDTH // 2
D_FF = 2 * D_MODEL
N_MOD = 9
ROPE_THETA = 10000.0
EPS = 1e-6
KV_COLS = 2 * KV_WIDTH
Q_END = KV_COLS + ATTN_WIDTH
GLU_END = Q_END + 2 * CONV_CH

M_LAT = BATCH * SEQ
M_CTX = BATCH * CTX_LEN
M_ALL = M_LAT + M_CTX
SEG_ROWS = 8

VMEM_LIMIT_BYTES = 56 * 1024 * 1024
CAST_ROWS = 512

F32 = jnp.float32
BF16 = jnp.bfloat16


def _params(n_axes):
    return pltpu.CompilerParams(dimension_semantics=("arbitrary",) * n_axes,
                                vmem_limit_bytes=VMEM_LIMIT_BYTES)


def _seg_select(p, rows):
    return jnp.where(rows < SEQ, p[0:1, :], jnp.where(rows < M_LAT, p[1:2, :], p[2:3, :]))


def _sigmoid(x):
    return 1.0 / (1.0 + jnp.exp(-x))


def _fused_matmul_kernel(*refs, n_x, w_x, e_kinds, n_o, bm, row_split, epilogue, x_prologue, use_scratch):
    n_w = len(w_x)
    n_e = len(e_kinds)
    x_refs = refs[:n_x]
    w_refs = refs[n_x:n_x + n_w]
    e_refs = refs[n_x + n_w:n_x + n_w + n_e]
    o_refs = refs[n_x + n_w + n_e:n_x + n_w + n_e + n_o]
    s_refs = refs[n_x + n_w + n_e + n_o:]
    i = pl.program_id(1)

    if use_scratch:
        @pl.when(i == 0)
        def _():
            for w_ref, s_ref in zip(w_refs, s_refs):
                def body(t, carry, w_ref=w_ref, s_ref=s_ref):
                    r = pl.ds(pl.multiple_of(t * CAST_ROWS, CAST_ROWS), CAST_ROWS)
                    s_ref[r, :] = w_ref[r, :].astype(BF16)
                    return carry
                lax.fori_loop(0, w_ref.shape[0] // CAST_ROWS, body, 0)
        w_vals = [s_ref[...] for s_ref in s_refs]
    else:
        w_vals = [w_ref[...].astype(BF16) for w_ref in w_refs]

    sub = bm // row_split
    for r in range(row_split):
        rs = slice(r * sub, (r + 1) * sub)
        xs = [x_ref[rs, :] for x_ref in x_refs]
        if x_prologue is not None:
            xs = [x_prologue(v).astype(BF16) for v in xs]
        accs = [jnp.dot(xs[xi], w, preferred_element_type=F32) for xi, w in zip(w_x, w_vals)]
        rows = i * bm + r * sub + lax.broadcasted_iota(jnp.int32, (sub, 1), 0)
        extras = [e_ref[rs, :] if kind in ("tile", "rows") else e_ref[...] for e_ref, kind in zip(e_refs, e_kinds)]
        outs = epilogue(accs, extras, rows)
        for o_ref, o in zip(o_refs, outs):
            o_ref[rs, :] = o.astype(o_ref.dtype)


def _fused_matmul(name, xs, ws, extras, epilogue, outs, *, m_rows, bm, bn, n_tiles, x_prologue=None,
                  k_width=None, k_block=0, row_split=4):
    assert m_rows % bm == 0
    n_m = m_rows // bm
    use_scratch = n_m > 1
    k_of = lambda a: a.shape[-1 if a.ndim == 2 else 1] if k_width is None else k_width
    in_specs = [pl.BlockSpec((bm, k_of(xa)), lambda j, i: (i, k_block)) for xa in xs]
    for wa, layer, off, _ in ws:
        assert k_of(wa) % CAST_ROWS == 0
        in_specs.append(pl.BlockSpec((None, k_of(wa), bn),
                                     lambda j, i, layer=layer, off=off: (layer, k_block, off + j)))
    for ea, kind, off in extras:
        if kind == "tile":
            in_specs.append(pl.BlockSpec((bm, bn), lambda j, i, off=off: (i, off + j)))
        elif kind == "rows":
            in_specs.append(pl.BlockSpec((bm, ea.shape[1]), lambda j, i: (i, 0)))
        elif kind == "seg":
            in_specs.append(pl.BlockSpec((SEG_ROWS, bn), lambda j, i, off=off: (0, off + j)))
        elif kind == "whole":
            in_specs.append(pl.BlockSpec(ea.shape, lambda j, i: (0, 0)))
        else:
            assert kind == "col"
            in_specs.append(pl.BlockSpec((1, bn), lambda j, i, off=off: (0, off + j)))
    out_specs = [pl.BlockSpec((bm, bn), lambda j, i: (i, j)) for _ in outs]
    out_shape = [jax.ShapeDtypeStruct((m_rows, n_tiles * bn), dt) for dt in outs]
    scratch = [pltpu.VMEM((k_of(wa), bn), BF16) for wa, _, _, _ in ws] if use_scratch else []
    body = functools.partial(
        _fused_matmul_kernel, n_x=len(xs), w_x=tuple(w[3] for w in ws), e_kinds=tuple(e[1] for e in extras),
        n_o=len(outs), bm=bm, row_split=row_split, epilogue=epilogue, x_prologue=x_prologue,
        use_scratch=use_scratch)
    res = pl.pallas_call(
        body, grid=(n_tiles, n_m), in_specs=in_specs, out_specs=out_specs, out_shape=out_shape,
        scratch_shapes=scratch, compiler_params=_params(2), name=name,
    )(*xs, *[w[0] for w in ws], *[e[0] for e in extras])
    return res


def _row_block(m_rows):
    return m_rows // 8


def _ep_bias(accs, extras, rows):
    return [accs[0] + extras[0]]


def _ep_swiglu(accs, extras, rows):
    a, b = accs
    return [a * _sigmoid(a) * b]


def _ep_residual(accs, extras, rows, *, weight):
    resid, gate = extras[:2]
    acc = accs[0] if len(extras) == 2 else extras[2] + accs[0]
    return [resid + (weight * _seg_select(gate, rows)) * acc]


def _ep_plain(accs, extras, rows):
    return [accs[0]]


def _ep_head_norm_rope(accs, extras, rows, *, scale):
    g, cosf, sinf = extras
    acc = accs[0]
    chunks = []
    for h in range(acc.shape[1] // HEAD_DIM):
        t = acc[:, h * HEAD_DIM:(h + 1) * HEAD_DIM]
        t = t * lax.rsqrt(jnp.mean(t * t, axis=-1, keepdims=True) + EPS) * g
        t = t * cosf + pltpu.roll(t, HEAD_DIM // 2, 1) * sinf
        chunks.append(t * scale if scale != 1.0 else t)
    return [jnp.concatenate(chunks, axis=1)]


def _ep_glu(accs, extras, rows):
    a, gt = accs
    return [a * _sigmoid(gt)]


def _ep_sigmoid(accs, extras, rows):
    return [_sigmoid(accs[0])]


def _ep_merge(accs, extras, rows):
    g_attn, g_conv = extras
    return [g_attn.astype(F32) * accs[0] + g_conv.astype(F32) * accs[1]]


def _silu(x):
    return x * _sigmoid(x)


NORM_ROWS = 256


def _norm_mod_kernel(h_ref, g_ref, shift_ref, scale_ref, o_ref):
    seg = pl.ds(jnp.minimum(pl.program_id(0) // (SEQ // NORM_ROWS), BATCH), 1)
    x = h_ref[...]
    inv = lax.rsqrt(jnp.mean(x * x, axis=-1, keepdims=True) + EPS)
    o_ref[...] = (x * inv * (g_ref[...] * (1.0 + scale_ref[seg, :])) + shift_ref[seg, :]).astype(BF16)


def _norm_mod(name, h, g, m_all, o, m_rows):
    return pl.pallas_call(
        _norm_mod_kernel, grid=(m_rows // NORM_ROWS,),
        in_specs=[pl.BlockSpec((NORM_ROWS, D_MODEL), lambda i: (i, 0)),
                  pl.BlockSpec((1, D_MODEL), lambda i: (0, 0)),
                  pl.BlockSpec((SEG_ROWS, D_MODEL), lambda i: (0, o)),
                  pl.BlockSpec((SEG_ROWS, D_MODEL), lambda i: (0, o + 1))],
        out_specs=pl.BlockSpec((NORM_ROWS, D_MODEL), lambda i: (i, 0)),
        out_shape=jax.ShapeDtypeStruct((m_rows, D_MODEL), BF16),
        compiler_params=_params(1), name=name,
    )(h, g.reshape(1, D_MODEL), m_all, m_all)


ATTN_Q_ROWS = 256
ATTN_KV_ROWS = 512
LAT_Q_BLOCKS = SEQ // ATTN_Q_ROWS


def _attn_kernel(q_ref, kl_ref, vl_ref, kc_ref, vc_ref, o_ref, m_sc, acc_sc, s0_sc, s1_sc):
    qi = pl.program_id(2)
    m_sc[...] = jnp.full(m_sc.shape, -jnp.inf, F32)
    acc_sc[...] = jnp.zeros(acc_sc.shape, F32)

    def scores(k):
        return tuple(
            lax.dot_general(q_ref[:, g * HEAD_DIM:(g + 1) * HEAD_DIM], k, (((1,), (1,)), ((), ())),
                            preferred_element_type=F32)
            for g in range(GROUP))

    def update(scores_kv, v):
        v_ones = jnp.concatenate([v, jnp.ones(v.shape, BF16)], axis=1)
        for g, s in enumerate(scores_kv):
            chunks = [s[:, c * LANES:(c + 1) * LANES] for c in range(s.shape[1] // LANES)]
            m_prev = m_sc[g]
            m_new = jnp.maximum(m_prev, jnp.max(functools.reduce(jnp.maximum, chunks), axis=-1, keepdims=True))
            alpha = jnp.exp(m_prev - m_new)
            p = jnp.concatenate([jnp.exp(c - m_new).astype(BF16) for c in chunks], axis=1)
            pv = jnp.dot(p, v_ones, preferred_element_type=F32)
            acc_sc[g] = jnp.concatenate([alpha, alpha], axis=1) * acc_sc[g] + pv
            m_sc[g] = m_new

    def lat_rows(t):
        return pl.ds(pl.multiple_of(t * ATTN_KV_ROWS, ATTN_KV_ROWS), ATTN_KV_ROWS)

    @pl.when(qi < LAT_Q_BLOCKS)
    def _():
        def put(sc, s_kv):
            for g, s in enumerate(s_kv):
                sc[g] = s

        def get(sc):
            return tuple(sc[g] for g in range(GROUP))

        def body(tt, carry):
            t = 2 * tt
            put(s1_sc, scores(kl_ref[lat_rows(t + 1), :]))
            update(get(s0_sc), vl_ref[lat_rows(t), :])
            put(s0_sc, scores(kl_ref[lat_rows(t + 2), :]))
            update(get(s1_sc), vl_ref[lat_rows(t + 1), :])
            return carry
        n_lat = SEQ // ATTN_KV_ROWS
        put(s0_sc, scores(kl_ref[lat_rows(0), :]))
        lax.fori_loop(0, n_lat // 2 - 1, body, 0)
        put(s1_sc, scores(kl_ref[lat_rows(n_lat - 1), :]))
        update(get(s0_sc), vl_ref[lat_rows(n_lat - 2), :])
        update(get(s1_sc), vl_ref[lat_rows(n_lat - 1), :])

    update(scores(kc_ref[...]), vc_ref[...])
    for g in range(GROUP):
        acc = acc_sc[g]
        o_ref[:, g * HEAD_DIM:(g + 1) * HEAD_DIM] = (acc[:, :HEAD_DIM] / acc[:, HEAD_DIM:]).astype(BF16)


def _attention(name, q, k, v, with_ctx_queries):
    ctx_q_blocks = CTX_LEN // ATTN_Q_ROWS
    n_q = LAT_Q_BLOCKS + (ctx_q_blocks if with_ctx_queries else 0)
    m_out = M_ALL if with_ctx_queries else M_LAT
    ctx_block0 = M_LAT // CTX_LEN

    def q_map(b, h, qi):
        ctx_q = M_LAT // ATTN_Q_ROWS + b * ctx_q_blocks + (qi - LAT_Q_BLOCKS)
        return (jnp.where(qi < LAT_Q_BLOCKS, b * LAT_Q_BLOCKS + qi, ctx_q), h)

    lat_spec = pl.BlockSpec((SEQ, HEAD_DIM), lambda b, h, qi: (b, h))
    ctx_spec = pl.BlockSpec((CTX_LEN, HEAD_DIM), lambda b, h, qi: (ctx_block0 + b, h))
    stats = pltpu.VMEM((GROUP, ATTN_Q_ROWS, HEAD_DIM), F32)
    return pl.pallas_call(
        _attn_kernel, grid=(BATCH, N_KV_HEADS, n_q),
        in_specs=[pl.BlockSpec((ATTN_Q_ROWS, GROUP * HEAD_DIM), q_map), lat_spec, lat_spec, ctx_spec, ctx_spec],
        out_specs=pl.BlockSpec((ATTN_Q_ROWS, GROUP * HEAD_DIM), q_map),
        out_shape=jax.ShapeDtypeStruct((m_out, ATTN_WIDTH), BF16),
        scratch_shapes=[stats, pltpu.VMEM((GROUP, ATTN_Q_ROWS, 2 * HEAD_DIM), F32),
                        pltpu.VMEM((GROUP, ATTN_Q_ROWS, ATTN_KV_ROWS), F32),
                        pltpu.VMEM((GROUP, ATTN_Q_ROWS, ATTN_KV_ROWS), F32)],
        compiler_params=_params(3), name=name,
    )(q, k, v, k, v)


CONV_ROWS = 256
HALO = 16
LANES = 128
SUBLANES = 8


def _conv_kernel(prev_ref, cur_ref, next_ref, w_ref, b_ref, g_ref, o_ref, pad_sc, y_sc):
    i = pl.program_id(0)
    lat_blocks = SEQ // CONV_ROWS
    n_lat = BATCH * lat_blocks
    first = (i % lat_blocks == 0) | (i >= n_lat)
    last = (i % lat_blocks == lat_blocks - 1) | (i >= n_lat)
    pad_sc[0:HALO, :] = jnp.where(first, 0.0, prev_ref[...].astype(F32))
    pad_sc[HALO:HALO + CONV_ROWS, :] = cur_ref[...].astype(F32)
    pad_sc[HALO + CONV_ROWS:, :] = jnp.where(last, 0.0, next_ref[...].astype(F32))

    def lanes(c):
        return pl.ds(pl.multiple_of(c * LANES, LANES), LANES)

    def chunk(c, ss):
        col = lanes(c)
        acc = jnp.broadcast_to(b_ref[:, col], (CONV_ROWS, LANES))
        for r in range(SUBLANES):
            part = None
            for k in range(CONV_WIDTH):
                start = HALO - CONV_PAD + k
                if start % SUBLANES != r:
                    continue
                base = start - r
                term = pad_sc[base:base + CONV_ROWS + SUBLANES, col] * w_ref[k:k + 1, col]
                part = term if part is None else part + term
            acc = acc + part[r:r + CONV_ROWS, :]
        y_sc[:, col] = acc
        return ss + acc * acc

    ss = lax.fori_loop(0, CONV_CH // LANES, chunk, jnp.zeros((CONV_ROWS, LANES), F32))
    inv = jnp.broadcast_to(lax.rsqrt(jnp.sum(ss, axis=-1, keepdims=True) * (1.0 / CONV_CH) + EPS),
                           (CONV_ROWS, LANES))

    def finish(c, carry):
        col = lanes(c)
        o_ref[:, col] = _silu(y_sc[:, col] * inv * g_ref[:, col]).astype(BF16)
        return carry

    lax.fori_loop(0, CONV_CH // LANES, finish, 0)


def _conv_module(name, u, w_dw, b_dw, g_norm, m_rows):
    assert CTX_LEN == CONV_ROWS
    per = CONV_ROWS // HALO
    n_halo = u.shape[0] // HALO
    return pl.pallas_call(
        _conv_kernel, grid=(m_rows // CONV_ROWS,),
        in_specs=[pl.BlockSpec((HALO, CONV_CH), lambda i: (jnp.maximum(i * per - 1, 0), 0)),
                  pl.BlockSpec((CONV_ROWS, CONV_CH), lambda i: (i, 0)),
                  pl.BlockSpec((HALO, CONV_CH), lambda i: (jnp.minimum((i + 1) * per, n_halo - 1), 0)),
                  pl.BlockSpec((CONV_WIDTH, CONV_CH), lambda i: (0, 0)),
                  pl.BlockSpec((1, CONV_CH), lambda i: (0, 0)),
                  pl.BlockSpec((1, CONV_CH), lambda i: (0, 0))],
        out_specs=pl.BlockSpec((CONV_ROWS, CONV_CH), lambda i: (i, 0)),
        out_shape=jax.ShapeDtypeStruct((m_rows, CONV_CH), BF16),
        scratch_shapes=[pltpu.VMEM((CONV_ROWS + 2 * HALO, CONV_CH), F32), pltpu.VMEM((CONV_ROWS, CONV_CH), F32)],
        compiler_params=_params(1), name=name,
    )(u, u, u, w_dw, b_dw.reshape(1, CONV_CH), g_norm.reshape(1, CONV_CH))


def _rope_tables():
    rows = SEQ // GRID_W
    row_idx = jnp.repeat(jnp.arange(rows, dtype=F32), GRID_W)
    col_idx = jnp.tile(jnp.arange(GRID_W, dtype=F32), rows)
    axis_dim = HEAD_DIM // 2
    inv_freq = ROPE_THETA ** (-jnp.arange(0, axis_dim, 2, dtype=F32) / axis_dim)
    ang = jnp.concatenate([row_idx[:, None] * inv_freq, col_idx[:, None] * inv_freq], axis=-1)
    cos, sin = jnp.cos(ang), jnp.sin(ang)
    cosf = jnp.concatenate([cos, cos], axis=-1)
    sinf = jnp.concatenate([-sin, sin], axis=-1)
    cosf = jnp.concatenate([cosf, cosf, jnp.ones((M_CTX, HEAD_DIM), F32)], axis=0)
    sinf = jnp.concatenate([sinf, sinf, jnp.zeros((M_CTX, HEAD_DIM), F32)], axis=0)
    return cosf, sinf


def _half_ffn(tag, h, m_all, g, w_in, w_out, layer, o, m_rows):
    hn = _norm_mod(f"{tag}_norm", h, g, m_all, o, m_rows)
    bm = _row_block(m_rows)
    bn = 256
    (u,) = _fused_matmul(f"{tag}_in", [hn], [(w_in, layer, 0, 0), (w_in, layer, D_FF // bn, 0)], [],
                         _ep_swiglu, [BF16], m_rows=m_rows, bm=bm, bn=bn, n_tiles=D_FF // bn)
    bn = 512
    half = D_FF // 2
    (part,) = _fused_matmul(f"{tag}_out_a", [u], [(w_out, layer, 0, 0)], [], _ep_plain, [F32],
                            m_rows=m_rows, bm=bm, bn=bn, n_tiles=D_MODEL // bn, k_width=half, k_block=0)
    (h_new,) = _fused_matmul(
        f"{tag}_out_b", [u], [(w_out, layer, 0, 0)],
        [(h, "tile", 0), (m_all, "seg", (o + 2) * D_MODEL // bn), (part, "tile", 0)],
        functools.partial(_ep_residual, weight=0.5), [F32],
        m_rows=m_rows, bm=bm, bn=bn, n_tiles=D_MODEL // bn, k_width=half, k_block=1)
    return h_new


def kernel(x, c, ctx, c_ctx, w_mod, b_mod, norm_ffn1, w_ffn1_in, w_ffn1_out, norm_mix, w_in, q_norm, k_norm,
           w_attn_o, conv_dw, conv_b, conv_norm, w_conv_o, w_out, norm_ffn2, w_ffn2_in, w_ffn2_out):
    h = jnp.concatenate([x.reshape(M_LAT, D_MODEL), ctx.reshape(M_CTX, D_MODEL)], axis=0)
    cvec = jnp.concatenate([c, c_ctx[None, :], jnp.zeros((SEG_ROWS - BATCH - 1, D_MODEL), F32)], axis=0)
    cosf, sinf = _rope_tables()
    bn = 512

    for layer in range(DEPTH):
        tag = f"l{layer}"
        update_ctx = layer < DEPTH - 1
        (m_all,) = _fused_matmul(
            f"{tag}_mod", [cvec], [(w_mod, layer, 0, 0)], [(b_mod[layer].reshape(1, -1), "col", 0)],
            _ep_bias, [F32], m_rows=SEG_ROWS, bm=SEG_ROWS, bn=bn, n_tiles=N_MOD * D_MODEL // bn, x_prologue=_silu,
            row_split=1)

        h = _half_ffn(f"{tag}_ffn1", h, m_all, norm_ffn1[layer], w_ffn1_in, w_ffn1_out, layer, 0, M_ALL)

        hn = _norm_mod(f"{tag}_mix_norm", h, norm_mix[layer], m_all, 3, M_ALL)
        m_mix = M_ALL if update_ctx else M_LAT
        rope = [(cosf, "rows", 0), (sinf, "rows", 0)]
        (k,) = _fused_matmul(
            f"{tag}_k", [hn], [(w_in, layer, 0, 0)], [(k_norm[layer].reshape(1, HEAD_DIM), "whole", 0)] + rope,
            functools.partial(_ep_head_norm_rope, scale=1.0), [BF16],
            m_rows=M_ALL, bm=_row_block(M_ALL), bn=bn, n_tiles=KV_WIDTH // bn, row_split=4)
        (v,) = _fused_matmul(
            f"{tag}_v", [hn], [(w_in, layer, KV_WIDTH // bn, 0)], [], _ep_plain, [BF16],
            m_rows=M_ALL, bm=_row_block(M_ALL), bn=bn, n_tiles=KV_WIDTH // bn)
        bm = _row_block(m_mix)
        (q,) = _fused_matmul(
            f"{tag}_q", [hn], [(w_in, layer, KV_COLS // bn, 0)],
            [(q_norm[layer].reshape(1, HEAD_DIM), "whole", 0)] + rope,
            functools.partial(_ep_head_norm_rope, scale=HEAD_DIM ** -0.5), [BF16],
            m_rows=m_mix, bm=bm, bn=bn, n_tiles=ATTN_WIDTH // bn, row_split=4)
        bg = 256
        (u,) = _fused_matmul(
            f"{tag}_glu", [hn], [(w_in, layer, Q_END // bg, 0), (w_in, layer, (Q_END + CONV_CH) // bg, 0)], [],
            _ep_glu, [BF16], m_rows=m_mix, bm=bm, bn=bg, n_tiles=CONV_CH // bg)
        (gates,) = _fused_matmul(
            f"{tag}_gates", [hn], [(w_in, layer, GLU_END // bn, 0)], [], _ep_sigmoid, [BF16],
            m_rows=m_mix, bm=bm, bn=bn, n_tiles=2 * D_MODEL // bn)

        attn = _attention(f"{tag}_attn", q, k, v, update_ctx)
        cv = _conv_module(f"{tag}_conv", u, conv_dw[layer], conv_b[layer], conv_norm[layer], m_mix)
        (merged,) = _fused_matmul(
            f"{tag}_merge", [attn, cv], [(w_attn_o, layer, 0, 0), (w_conv_o, layer, 0, 1)],
            [(gates, "tile", 0), (gates, "tile", D_MODEL // bn)], _ep_merge, [BF16],
            m_rows=m_mix, bm=bm, bn=bn, n_tiles=D_MODEL // bn)
        (h,) = _fused_matmul(
            f"{tag}_out", [merged], [(w_out, layer, 0, 0)],
            [(h, "tile", 0), (m_all, "seg", 5 * D_MODEL // bn)],
            functools.partial(_ep_residual, weight=1.0), [F32],
            m_rows=m_mix, bm=bm, bn=bn, n_tiles=D_MODEL // bn)

        h = _half_ffn(f"{tag}_ffn2", h, m_all, norm_ffn2[layer], w_ffn2_in, w_ffn2_out, layer, 6, m_mix)

    return h.reshape(BATCH, SEQ, D_MODEL)
```

```python
import functools

import jax
import jax.numpy as jnp
from jax import lax
from jax.experimental import pallas as pl
from jax.experimental.pallas import tpu as pltpu

D_MODEL = 4096
BATCH = 2
SEQ = 4096
DEPTH = 2
GRID_W = 64
CTX_LEN = 256
HEAD_DIM = 128
N_Q_HEADS = 16
N_KV_HEADS = 4
GROUP = N_Q_HEADS // N_KV_HEADS
ATTN_WIDTH = N_Q_HEADS * HEAD_DIM
KV_WIDTH = N_KV_HEADS * HEAD_DIM
CONV_CH = D_MODEL // 2
CONV_WIDTH = 31
CONV_PAD = CONV_WIDTH // 2
D_FF = 2 * D_MODEL
N_MOD = 9
ROPE_THETA = 10000.0
EPS = 1e-6
KV_COLS = 2 * KV_WIDTH
Q_END = KV_COLS + ATTN_WIDTH
GLU_END = Q_END + 2 * CONV_CH

M_LAT = BATCH * SEQ
M_CTX = BATCH * CTX_LEN
M_ALL = M_LAT + M_CTX
SEG_ROWS = 8

VMEM_LIMIT_BYTES = 56 * 1024 * 1024

F32 = jnp.float32
BF16 = jnp.bfloat16


def _params(n_axes):
    return pltpu.CompilerParams(dimension_semantics=("arbitrary",) * n_axes,
                                vmem_limit_bytes=VMEM_LIMIT_BYTES)


def _seg_select(p, rows):
    return jnp.where(rows < SEQ, p[0:1, :], jnp.where(rows < M_LAT, p[1:2, :], p[2:3, :]))


def _sigmoid(x):
    return 1.0 / (1.0 + jnp.exp(-x))


def _aligned(v, m):
    return v if isinstance(v, int) else pl.multiple_of(v, m)


def _fused_matmul_kernel(*refs, n_x, w_meta, e_kinds, n_o, bm, bn, n_m, n_tiles, k_rows, row_split, epilogue,
                         x_prologue, stream):
    n_w = len(w_meta)
    n_e = len(e_kinds)
    x_refs = refs[:n_x]
    w_refs = refs[n_x:n_x + n_w]
    e_refs = refs[n_x + n_w:n_x + n_w + n_e]
    o_refs = refs[n_x + n_w + n_e:n_x + n_w + n_e + n_o]
    scratch = refs[n_x + n_w + n_e + n_o:]
    j = pl.program_id(0)
    i = pl.program_id(1)

    if stream:
        wbf_refs, stage_refs, sems = scratch[:n_w], scratch[n_w:2 * n_w], scratch[2 * n_w]
        kc = k_rows // n_m

        def chunk_copy(w, tile, c, slot):
            _, layer, row0, off = w_meta[w]
            src = w_refs[w].at[layer, pl.ds(_aligned(row0 + c * kc, kc), kc),
                               pl.ds(_aligned((off + tile) * bn, bn), bn)]
            return pltpu.make_async_copy(src, stage_refs[w].at[slot], sems.at[w, slot])

        def cast_chunk(w, half, c, slot):
            wbf_refs[w][half, pl.ds(_aligned(c * kc, kc), kc), :] = stage_refs[w][slot].astype(BF16)

        @pl.when((j == 0) & (i == 0))
        def _():
            for w in range(n_w):
                chunk_copy(w, 0, 0, 0).start()
            for c in range(n_m):
                for w in range(n_w):
                    if c + 1 < n_m:
                        chunk_copy(w, 0, c + 1, (c + 1) % 2).start()
                    chunk_copy(w, 0, c, c % 2).wait()
                    cast_chunk(w, 0, c, c % 2)

        prefetch = j + 1 < n_tiles

        @pl.when(prefetch)
        def _():
            for w in range(n_w):
                chunk_copy(w, j + 1, i, 0).start()

        cur = j % 2
        w_vals = [wbf_ref[cur] for wbf_ref in wbf_refs]
    else:
        w_vals = [w_ref[...].astype(BF16) for w_ref in w_refs]
    w_x = [m[0] for m in w_meta]

    sub = bm // row_split
    for r in range(row_split):
        rs = slice(r * sub, (r + 1) * sub)
        xs = [x_ref[rs, :] for x_ref in x_refs]
        if x_prologue is not None:
            xs = [x_prologue(v).astype(BF16) for v in xs]
        accs = [jnp.dot(xs[xi], w, preferred_element_type=F32) for xi, w in zip(w_x, w_vals)]
        rows = i * bm + r * sub + lax.broadcasted_iota(jnp.int32, (sub, 1), 0)
        extras = [e_ref[rs, :] if kind in ("tile", "rows") else e_ref[...] for e_ref, kind in zip(e_refs, e_kinds)]
        outs = epilogue(accs, extras, rows)
        for o_ref, o in zip(o_refs, outs):
            o_ref[rs, :] = o.astype(o_ref.dtype)

    if stream:
        @pl.when(prefetch)
        def _():
            for w in range(n_w):
                chunk_copy(w, j + 1, i, 0).wait()
                cast_chunk(w, 1 - cur, i, 0)


def _fused_matmul(name, xs, ws, extras, epilogue, outs, *, m_rows, bm, bn, n_tiles, x_prologue=None,
                  k_width=None, k_block=0, row_split=4):
    assert m_rows % bm == 0
    n_m = m_rows // bm
    stream = n_m > 1
    k_of = lambda a: a.shape[-1 if a.ndim == 2 else 1] if k_width is None else k_width
    k_rows = k_of(ws[0][0])
    assert all(k_of(w[0]) == k_rows for w in ws) and all(k_of(xa) == k_rows for xa in xs)
    in_specs = [pl.BlockSpec((bm, k_rows), lambda j, i: (i, k_block)) for xa in xs]
    for wa, layer, off, _ in ws:
        if stream:
            assert k_rows % n_m == 0 and (k_rows // n_m) % 16 == 0
            in_specs.append(pl.BlockSpec(memory_space=pl.ANY))
        else:
            in_specs.append(pl.BlockSpec((None, k_rows, bn),
                                         lambda j, i, layer=layer, off=off: (layer, k_block, off + j)))
    for ea, kind, off in extras:
        if kind == "tile":
            in_specs.append(pl.BlockSpec((bm, bn), lambda j, i, off=off: (i, off + j)))
        elif kind == "rows":
            in_specs.append(pl.BlockSpec((bm, ea.shape[1]), lambda j, i: (i, 0)))
        elif kind == "seg":
            in_specs.append(pl.BlockSpec((SEG_ROWS, bn), lambda j, i, off=off: (0, off + j)))
        elif kind == "whole":
            in_specs.append(pl.BlockSpec(ea.shape, lambda j, i: (0, 0)))
        else:
            assert kind == "col"
            in_specs.append(pl.BlockSpec((1, bn), lambda j, i, off=off: (0, off + j)))
    out_specs = [pl.BlockSpec((bm, bn), lambda j, i: (i, j)) for _ in outs]
    out_shape = [jax.ShapeDtypeStruct((m_rows, n_tiles * bn), dt) for dt in outs]
    scratch = []
    if stream:
        scratch = ([pltpu.VMEM((2, k_rows, bn), BF16) for _ in ws]
                   + [pltpu.VMEM((2, k_rows // n_m, bn), F32) for _ in ws]
                   + [pltpu.SemaphoreType.DMA((len(ws), 2))])
    w_meta = tuple((xi, layer, k_block * k_rows, off) for _, layer, off, xi in ws)
    body = functools.partial(
        _fused_matmul_kernel, n_x=len(xs), w_meta=w_meta, e_kinds=tuple(e[1] for e in extras),
        n_o=len(outs), bm=bm, bn=bn, n_m=n_m, n_tiles=n_tiles, k_rows=k_rows, row_split=row_split,
        epilogue=epilogue, x_prologue=x_prologue, stream=stream)
    res = pl.pallas_call(
        body, grid=(n_tiles, n_m), in_specs=in_specs, out_specs=out_specs, out_shape=out_shape,
        scratch_shapes=scratch, compiler_params=_params(2), name=name,
    )(*xs, *[w[0] for w in ws], *[e[0] for e in extras])
    return res


def _row_block(m_rows):
    return m_rows // 8


BN_WIDE = 1024
BN_PAIR = 512
BN_F32_EXTRAS = 512


def _ep_bias(accs, extras, rows):
    return [accs[0] + extras[0]]


def _ep_swiglu(accs, extras, rows):
    a, b = accs
    return [a * _sigmoid(a) * b]


def _ep_residual(accs, extras, rows, *, weight):
    resid, gate = extras[:2]
    acc = accs[0] if len(extras) == 2 else extras[2] + accs[0]
    return [resid + (weight * _seg_select(gate, rows)) * acc]


def _ep_plain(accs, extras, rows):
    return [accs[0]]


def _ep_head_norm_rope(accs, extras, rows, *, scale):
    g, cosf, sinf = extras
    acc = accs[0]
    chunks = []
    for h in range(acc.shape[1] // HEAD_DIM):
        t = acc[:, h * HEAD_DIM:(h + 1) * HEAD_DIM]
        t = t * lax.rsqrt(jnp.mean(t * t, axis=-1, keepdims=True) + EPS) * g
        t = t * cosf + pltpu.roll(t, HEAD_DIM // 2, 1) * sinf
        chunks.append(t * scale if scale != 1.0 else t)
    return [jnp.concatenate(chunks, axis=1)]


def _ep_glu(accs, extras, rows):
    a, gt = accs
    return [a * _sigmoid(gt)]


def _ep_sigmoid(accs, extras, rows):
    return [_sigmoid(accs[0])]


def _ep_merge(accs, extras, rows):
    g_attn, g_conv = extras
    return [g_attn.astype(F32) * accs[0] + g_conv.astype(F32) * accs[1]]


def _silu(x):
    return x * _sigmoid(x)


NORM_ROWS = 256


def _norm_mod_kernel(h_ref, g_ref, shift_ref, scale_ref, o_ref):
    seg = pl.ds(jnp.minimum(pl.program_id(0) // (SEQ // NORM_ROWS), BATCH), 1)
    x = h_ref[...]
    inv = lax.rsqrt(jnp.mean(x * x, axis=-1, keepdims=True) + EPS)
    o_ref[...] = (x * inv * (g_ref[...] * (1.0 + scale_ref[seg, :])) + shift_ref[seg, :]).astype(BF16)


def _norm_mod(name, h, g, m_all, o, m_rows):
    return pl.pallas_call(
        _norm_mod_kernel, grid=(m_rows // NORM_ROWS,),
        in_specs=[pl.BlockSpec((NORM_ROWS, D_MODEL), lambda i: (i, 0)),
                  pl.BlockSpec((1, D_MODEL), lambda i: (0, 0)),
                  pl.BlockSpec((SEG_ROWS, D_MODEL), lambda i: (0, o)),
                  pl.BlockSpec((SEG_ROWS, D_MODEL), lambda i: (0, o + 1))],
        out_specs=pl.BlockSpec((NORM_ROWS, D_MODEL), lambda i: (i, 0)),
        out_shape=jax.ShapeDtypeStruct((m_rows, D_MODEL), BF16),
        compiler_params=_params(1), name=name,
    )(h, g.reshape(1, D_MODEL), m_all, m_all)


ATTN_Q_ROWS = 256
ATTN_KV_ROWS = 512
LAT_Q_BLOCKS = SEQ // ATTN_Q_ROWS


def _attn_kernel(q_ref, kl_ref, vl_ref, kc_ref, vc_ref, o_ref, m_sc, acc_sc, s0_sc, s1_sc):
    qi = pl.program_id(2)
    m_sc[...] = jnp.full(m_sc.shape, -jnp.inf, F32)
    acc_sc[...] = jnp.zeros(acc_sc.shape, F32)

    def scores(k):
        return tuple(
            lax.dot_general(q_ref[:, g * HEAD_DIM:(g + 1) * HEAD_DIM], k, (((1,), (1,)), ((), ())),
                            preferred_element_type=F32)
            for g in range(GROUP))

    def update(scores_kv, v):
        v_ones = jnp.concatenate([v, jnp.ones(v.shape, BF16)], axis=1)
        for g, s in enumerate(scores_kv):
            chunks = [s[:, c * LANES:(c + 1) * LANES] for c in range(s.shape[1] // LANES)]
            m_prev = m_sc[g]
            m_new = jnp.maximum(m_prev, jnp.max(functools.reduce(jnp.maximum, chunks), axis=-1, keepdims=True))
            alpha = jnp.exp(m_prev - m_new)
            p = jnp.concatenate([jnp.exp(c - m_new).astype(BF16) for c in chunks], axis=1)
            pv = jnp.dot(p, v_ones, preferred_element_type=F32)
            acc_sc[g] = jnp.concatenate([alpha, alpha], axis=1) * acc_sc[g] + pv
            m_sc[g] = m_new

    def lat_rows(t):
        return pl.ds(pl.multiple_of(t * ATTN_KV_ROWS, ATTN_KV_ROWS), ATTN_KV_ROWS)

    @pl.when(qi < LAT_Q_BLOCKS)
    def _():
        def put(sc, s_kv):
            for g, s in enumerate(s_kv):
                sc[g] = s

        def get(sc):
            return tuple(sc[g] for g in range(GROUP))

        def body(tt, carry):
            t = 2 * tt
            put(s1_sc, scores(kl_ref[lat_rows(t + 1), :]))
            update(get(s0_sc), vl_ref[lat_rows(t), :])
            put(s0_sc, scores(kl_ref[lat_rows(t + 2), :]))
            update(get(s1_sc), vl_ref[lat_rows(t + 1), :])
            return carry
        n_lat = SEQ // ATTN_KV_ROWS
        put(s0_sc, scores(kl_ref[lat_rows(0), :]))
        lax.fori_loop(0, n_lat // 2 - 1, body, 0)
        put(s1_sc, scores(kl_ref[lat_rows(n_lat - 1), :]))
        update(get(s0_sc), vl_ref[lat_rows(n_lat - 2), :])
        update(get(s1_sc), vl_ref[lat_rows(n_lat - 1), :])

    update(scores(kc_ref[...]), vc_ref[...])
    for g in range(GROUP):
        acc = acc_sc[g]
        o_ref[:, g * HEAD_DIM:(g + 1) * HEAD_DIM] = (acc[:, :HEAD_DIM] / acc[:, HEAD_DIM:]).astype(BF16)


def _attention(name, q, k, v, with_ctx_queries):
    ctx_q_blocks = CTX_LEN // ATTN_Q_ROWS
    n_q = LAT_Q_BLOCKS + (ctx_q_blocks if with_ctx_queries else 0)
    m_out = M_ALL if with_ctx_queries else M_LAT
    ctx_block0 = M_LAT // CTX_LEN

    def q_map(b, h, qi):
        ctx_q = M_LAT // ATTN_Q_ROWS + b * ctx_q_blocks + (qi - LAT_Q_BLOCKS)
        return (jnp.where(qi < LAT_Q_BLOCKS, b * LAT_Q_BLOCKS + qi, ctx_q), h)

    lat_spec = pl.BlockSpec((SEQ, HEAD_DIM), lambda b, h, qi: (b, h))
    ctx_spec = pl.BlockSpec((CTX_LEN, HEAD_DIM), lambda b, h, qi: (ctx_block0 + b, h))
    stats = pltpu.VMEM((GROUP, ATTN_Q_ROWS, HEAD_DIM), F32)
    return pl.pallas_call(
        _attn_kernel, grid=(BATCH, N_KV_HEADS, n_q),
        in_specs=[pl.BlockSpec((ATTN_Q_ROWS, GROUP * HEAD_DIM), q_map), lat_spec, lat_spec, ctx_spec, ctx_spec],
        out_specs=pl.BlockSpec((ATTN_Q_ROWS, GROUP * HEAD_DIM), q_map),
        out_shape=jax.ShapeDtypeStruct((m_out, ATTN_WIDTH), BF16),
        scratch_shapes=[stats, pltpu.VMEM((GROUP, ATTN_Q_ROWS, 2 * HEAD_DIM), F32),
                        pltpu.VMEM((GROUP, ATTN_Q_ROWS, ATTN_KV_ROWS), F32),
                        pltpu.VMEM((GROUP, ATTN_Q_ROWS, ATTN_KV_ROWS), F32)],
        compiler_params=_params(3), name=name,
    )(q, k, v, k, v)


CONV_ROWS = 256
HALO = 16
LANES = 128
SUBLANES = 8


def _conv_kernel(prev_ref, cur_ref, next_ref, w_ref, b_ref, g_ref, o_ref, pad_sc, y_sc):
    i = pl.program_id(0)
    lat_blocks = SEQ // CONV_ROWS
    n_lat = BATCH * lat_blocks
    first = (i % lat_blocks == 0) | (i >= n_lat)
    last = (i % lat_blocks == lat_blocks - 1) | (i >= n_lat)
    pad_sc[0:HALO, :] = jnp.where(first, 0.0, prev_ref[...].astype(F32))
    pad_sc[HALO:HALO + CONV_ROWS, :] = cur_ref[...].astype(F32)
    pad_sc[HALO + CONV_ROWS:, :] = jnp.where(last, 0.0, next_ref[...].astype(F32))

    def lanes(c):
        return pl.ds(pl.multiple_of(c * LANES, LANES), LANES)

    def chunk(c, ss):
        col = lanes(c)
        acc = jnp.broadcast_to(b_ref[:, col], (CONV_ROWS, LANES))
        for r in range(SUBLANES):
            part = None
            for k in range(CONV_WIDTH):
                start = HALO - CONV_PAD + k
                if start % SUBLANES != r:
                    continue
                base = start - r
                term = pad_sc[base:base + CONV_ROWS + SUBLANES, col] * w_ref[k:k + 1, col]
                part = term if part is None else part + term
            acc = acc + part[r:r + CONV_ROWS, :]
        y_sc[:, col] = acc
        return ss + acc * acc

    ss = lax.fori_loop(0, CONV_CH // LANES, chunk, jnp.zeros((CONV_ROWS, LANES), F32))
    inv = jnp.broadcast_to(lax.rsqrt(jnp.sum(ss, axis=-1, keepdims=True) * (1.0 / CONV_CH) + EPS),
                           (CONV_ROWS, LANES))

    def finish(c, carry):
        col = lanes(c)
        o_ref[:, col] = _silu(y_sc[:, col] * inv * g_ref[:, col]).astype(BF16)
        return carry

    lax.fori_loop(0, CONV_CH // LANES, finish, 0)


def _conv_module(name, u, w_dw, b_dw, g_norm, m_rows):
    assert CTX_LEN == CONV_ROWS
    per = CONV_ROWS // HALO
    n_halo = u.shape[0] // HALO
    return pl.pallas_call(
        _conv_kernel, grid=(m_rows // CONV_ROWS,),
        in_specs=[pl.BlockSpec((HALO, CONV_CH), lambda i: (jnp.maximum(i * per - 1, 0), 0)),
                  pl.BlockSpec((CONV_ROWS, CONV_CH), lambda i: (i, 0)),
                  pl.BlockSpec((HALO, CONV_CH), lambda i: (jnp.minimum((i + 1) * per, n_halo - 1), 0)),
                  pl.BlockSpec((CONV_WIDTH, CONV_CH), lambda i: (0, 0)),
                  pl.BlockSpec((1, CONV_CH), lambda i: (0, 0)),
                  pl.BlockSpec((1, CONV_CH), lambda i: (0, 0))],
        out_specs=pl.BlockSpec((CONV_ROWS, CONV_CH), lambda i: (i, 0)),
        out_shape=jax.ShapeDtypeStruct((m_rows, CONV_CH), BF16),
        scratch_shapes=[pltpu.VMEM((CONV_ROWS + 2 * HALO, CONV_CH), F32), pltpu.VMEM((CONV_ROWS, CONV_CH), F32)],
        compiler_params=_params(1), name=name,
    )(u, u, u, w_dw, b_dw.reshape(1, CONV_CH), g_norm.reshape(1, CONV_CH))


def _rope_tables():
    rows = SEQ // GRID_W
    row_idx = jnp.repeat(jnp.arange(rows, dtype=F32), GRID_W)
    col_idx = jnp.tile(jnp.arange(GRID_W, dtype=F32), rows)
    axis_dim = HEAD_DIM // 2
    inv_freq = ROPE_THETA ** (-jnp.arange(0, axis_dim, 2, dtype=F32) / axis_dim)
    ang = jnp.concatenate([row_idx[:, None] * inv_freq, col_idx[:, None] * inv_freq], axis=-1)
    cos, sin = jnp.cos(ang), jnp.sin(ang)
    cosf = jnp.concatenate([cos, cos], axis=-1)
    sinf = jnp.concatenate([-sin, sin], axis=-1)
    cosf = jnp.concatenate([cosf, cosf, jnp.ones((M_CTX, HEAD_DIM), F32)], axis=0)
    sinf = jnp.concatenate([sinf, sinf, jnp.zeros((M_CTX, HEAD_DIM), F32)], axis=0)
    return cosf, sinf


def _half_ffn(tag, h, m_all, g, w_in, w_out, layer, o, m_rows):
    hn = _norm_mod(f"{tag}_norm", h, g, m_all, o, m_rows)
    bm = _row_block(m_rows)
    bn = BN_PAIR
    (u,) = _fused_matmul(f"{tag}_in", [hn], [(w_in, layer, 0, 0), (w_in, layer, D_FF // bn, 0)], [],
                         _ep_swiglu, [BF16], m_rows=m_rows, bm=bm, bn=bn, n_tiles=D_FF // bn)
    half = D_FF // 2
    bn = BN_WIDE
    (part,) = _fused_matmul(f"{tag}_out_a", [u], [(w_out, layer, 0, 0)], [], _ep_plain, [F32],
                            m_rows=m_rows, bm=bm, bn=bn, n_tiles=D_MODEL // bn, k_width=half, k_block=0)
    bn = BN_F32_EXTRAS
    (h_new,) = _fused_matmul(
        f"{tag}_out_b", [u], [(w_out, layer, 0, 0)],
        [(h, "tile", 0), (m_all, "seg", (o + 2) * D_MODEL // bn), (part, "tile", 0)],
        functools.partial(_ep_residual, weight=0.5), [F32],
        m_rows=m_rows, bm=bm, bn=bn, n_tiles=D_MODEL // bn, k_width=half, k_block=1)
    return h_new


def kernel(x, c, ctx, c_ctx, w_mod, b_mod, norm_ffn1, w_ffn1_in, w_ffn1_out, norm_mix, w_in, q_norm, k_norm,
           w_attn_o, conv_dw, conv_b, conv_norm, w_conv_o, w_out, norm_ffn2, w_ffn2_in, w_ffn2_out):
    h = jnp.concatenate([x.reshape(M_LAT, D_MODEL), ctx.reshape(M_CTX, D_MODEL)], axis=0)
    cvec = jnp.concatenate([c, c_ctx[None, :], jnp.zeros((SEG_ROWS - BATCH - 1, D_MODEL), F32)], axis=0)
    cosf, sinf = _rope_tables()
    bn = BN_F32_EXTRAS

    for layer in range(DEPTH):
        tag = f"l{layer}"
        update_ctx = layer < DEPTH - 1
        (m_all,) = _fused_matmul(
            f"{tag}_mod", [cvec], [(w_mod, layer, 0, 0)], [(b_mod[layer].reshape(1, -1), "col", 0)],
            _ep_bias, [F32], m_rows=SEG_ROWS, bm=SEG_ROWS, bn=bn, n_tiles=N_MOD * D_MODEL // bn, x_prologue=_silu,
            row_split=1)

        h = _half_ffn(f"{tag}_ffn1", h, m_all, norm_ffn1[layer], w_ffn1_in, w_ffn1_out, layer, 0, M_ALL)

        hn = _norm_mod(f"{tag}_mix_norm", h, norm_mix[layer], m_all, 3, M_ALL)
        m_mix = M_ALL if update_ctx else M_LAT
        rope = [(cosf, "rows", 0), (sinf, "rows", 0)]
        (k,) = _fused_matmul(
            f"{tag}_k", [hn], [(w_in, layer, 0, 0)], [(k_norm[layer].reshape(1, HEAD_DIM), "whole", 0)] + rope,
            functools.partial(_ep_head_norm_rope, scale=1.0), [BF16],
            m_rows=M_ALL, bm=_row_block(M_ALL), bn=bn, n_tiles=KV_WIDTH // bn, row_split=4)
        (v,) = _fused_matmul(
            f"{tag}_v", [hn], [(w_in, layer, KV_WIDTH // bn, 0)], [], _ep_plain, [BF16],
            m_rows=M_ALL, bm=_row_block(M_ALL), bn=bn, n_tiles=KV_WIDTH // bn)
        bm = _row_block(m_mix)
        bw = BN_WIDE
        (q,) = _fused_matmul(
            f"{tag}_q", [hn], [(w_in, layer, KV_COLS // bw, 0)],
            [(q_norm[layer].reshape(1, HEAD_DIM), "whole", 0)] + rope,
            functools.partial(_ep_head_norm_rope, scale=HEAD_DIM ** -0.5), [BF16],
            m_rows=m_mix, bm=bm, bn=bw, n_tiles=ATTN_WIDTH // bw, row_split=4)
        bg = BN_PAIR
        (u,) = _fused_matmul(
            f"{tag}_glu", [hn], [(w_in, layer, Q_END // bg, 0), (w_in, layer, (Q_END + CONV_CH) // bg, 0)], [],
            _ep_glu, [BF16], m_rows=m_mix, bm=bm, bn=bg, n_tiles=CONV_CH // bg)
        (gates,) = _fused_matmul(
            f"{tag}_gates", [hn], [(w_in, layer, GLU_END // bw, 0)], [], _ep_sigmoid, [BF16],
            m_rows=m_mix, bm=bm, bn=bw, n_tiles=2 * D_MODEL // bw)

        attn = _attention(f"{tag}_attn", q, k, v, update_ctx)
        cv = _conv_module(f"{tag}_conv", u, conv_dw[layer], conv_b[layer], conv_norm[layer], m_mix)
        (merged,) = _fused_matmul(
            f"{tag}_merge", [attn, cv], [(w_attn_o, layer, 0, 0), (w_conv_o, layer, 0, 1)],
            [(gates, "tile", 0), (gates, "tile", D_MODEL // bg)], _ep_merge, [BF16],
            m_rows=m_mix, bm=bm, bn=bg, n_tiles=D_MODEL // bg)
        (h,) = _fused_matmul(
            f"{tag}_out", [merged], [(w_out, layer, 0, 0)],
            [(h, "tile", 0), (m_all, "seg", 5 * D_MODEL // bn)],
            functools.partial(_ep_residual, weight=1.0), [F32],
            m_rows=m_mix, bm=bm, bn=bn, n_tiles=D_MODEL // bn)

        h = _half_ffn(f"{tag}_ffn2", h, m_all, norm_ffn2[layer], w_ffn2_in, w_ffn2_out, layer, 6, m_mix)

    return h.reshape(BATCH, SEQ, D_MODEL)
```

```python
import functools

import jax
import jax.numpy as jnp
from jax import lax
from jax.experimental import pallas as pl
from jax.experimental.pallas import tpu as pltpu

D_MODEL = 4096
BATCH = 2
SEQ = 4096
DEPTH = 2
GRID_W = 64
CTX_LEN = 256
HEAD_DIM = 128
N_Q_HEADS = 16
N_KV_HEADS = 4
GROUP = N_Q_HEADS // N_KV_HEADS
ATTN_WIDTH = N_Q_HEADS * HEAD_DIM
KV_WIDTH = N_KV_HEADS * HEAD_DIM
CONV_CH = D_MODEL // 2
CONV_WIDTH = 31
CONV_PAD = CONV_WIDTH // 2
D_FF = 2 * D_MODEL
N_MOD = 9
ROPE_THETA = 10000.0
EPS = 1e-6
KV_COLS = 2 * KV_WIDTH
Q_END = KV_COLS + ATTN_WIDTH
GLU_END = Q_END + 2 * CONV_CH

M_LAT = BATCH * SEQ
M_CTX = BATCH * CTX_LEN
M_ALL = M_LAT + M_CTX
SEG_ROWS = 8

VMEM_LIMIT_BYTES = 56 * 1024 * 1024

F32 = jnp.float32
BF16 = jnp.bfloat16


def _params(n_axes):
    return pltpu.CompilerParams(dimension_semantics=("arbitrary",) * n_axes,
                                vmem_limit_bytes=VMEM_LIMIT_BYTES)


def _seg_select(p, rows):
    return jnp.where(rows < SEQ, p[0:1, :], jnp.where(rows < M_LAT, p[1:2, :], p[2:3, :]))


def _sigmoid(x):
    return 1.0 / (1.0 + jnp.exp(-x))


STAGE_SLOTS = 3
PREFETCH_SLOT = 2


def _aligned(v, m):
    return v if isinstance(v, int) else pl.multiple_of(v, m)


def _fused_matmul_kernel(*refs, n_x, w_meta, e_kinds, n_o, bm, bn, n_m, n_tiles, k_rows, row_split, epilogue,
                         x_prologue, stream):
    n_w = len(w_meta)
    n_e = len(e_kinds)
    x_refs = refs[:n_x]
    w_refs = refs[n_x:n_x + n_w]
    e_refs = refs[n_x + n_w:n_x + n_w + n_e]
    o_refs = refs[n_x + n_w + n_e:n_x + n_w + n_e + n_o]
    scratch = refs[n_x + n_w + n_e + n_o:]
    j = pl.program_id(0)
    i = pl.program_id(1)

    if stream:
        wbf_refs, stage_refs, sems = scratch[:n_w], scratch[n_w:2 * n_w], scratch[2 * n_w]
        kc = k_rows // n_m

        def chunk_copy(w, tile, c, slot):
            _, layer, row0, off = w_meta[w]
            src = w_refs[w].at[layer, pl.ds(_aligned(row0 + c * kc, kc), kc),
                               pl.ds(_aligned((off + tile) * bn, bn), bn)]
            return pltpu.make_async_copy(src, stage_refs[w].at[slot], sems.at[w, slot])

        def cast_chunk(w, half, c, slot):
            wbf_refs[w][half, pl.ds(_aligned(c * kc, kc), kc), :] = stage_refs[w][slot].astype(BF16)

        prefetch = j + 1 < n_tiles

        @pl.when(prefetch)
        def _():
            for w in range(n_w):
                chunk_copy(w, j + 1, i, PREFETCH_SLOT).start()

        @pl.when((j == 0) & (i == 0))
        def _():
            for w in range(n_w):
                chunk_copy(w, 0, 0, 0).start()
            for c in range(n_m):
                for w in range(n_w):
                    if c + 1 < n_m:
                        chunk_copy(w, 0, c + 1, (c + 1) % 2).start()
                    chunk_copy(w, 0, c, c % 2).wait()
                    cast_chunk(w, 0, c, c % 2)

        cur = j % 2
        w_vals = [wbf_ref[cur] for wbf_ref in wbf_refs]
    else:
        w_vals = [w_ref[...].astype(BF16) for w_ref in w_refs]
    w_x = [m[0] for m in w_meta]

    sub = bm // row_split
    for r in range(row_split):
        rs = slice(r * sub, (r + 1) * sub)
        xs = [x_ref[rs, :] for x_ref in x_refs]
        if x_prologue is not None:
            xs = [x_prologue(v).astype(BF16) for v in xs]
        accs = [jnp.dot(xs[xi], w, preferred_element_type=F32) for xi, w in zip(w_x, w_vals)]
        rows = i * bm + r * sub + lax.broadcasted_iota(jnp.int32, (sub, 1), 0)
        extras = [e_ref[rs, :] if kind in ("tile", "rows") else e_ref[...] for e_ref, kind in zip(e_refs, e_kinds)]
        outs = epilogue(accs, extras, rows)
        for o_ref, o in zip(o_refs, outs):
            o_ref[rs, :] = o.astype(o_ref.dtype)

    if stream:
        @pl.when(prefetch)
        def _():
            for w in range(n_w):
                chunk_copy(w, j + 1, i, PREFETCH_SLOT).wait()
                cast_chunk(w, 1 - cur, i, PREFETCH_SLOT)


def _fused_matmul(name, xs, ws, extras, epilogue, outs, *, m_rows, bm, bn, n_tiles, x_prologue=None,
                  k_width=None, k_block=0, row_split=4):
    assert m_rows % bm == 0
    n_m = m_rows // bm
    stream = n_m > 1
    k_of = lambda a: a.shape[-1 if a.ndim == 2 else 1] if k_width is None else k_width
    k_rows = k_of(ws[0][0])
    assert all(k_of(w[0]) == k_rows for w in ws) and all(k_of(xa) == k_rows for xa in xs)
    in_specs = [pl.BlockSpec((bm, k_rows), lambda j, i: (i, k_block)) for xa in xs]
    for wa, layer, off, _ in ws:
        if stream:
            assert k_rows % n_m == 0 and (k_rows // n_m) % 16 == 0
            in_specs.append(pl.BlockSpec(memory_space=pl.ANY))
        else:
            in_specs.append(pl.BlockSpec((None, k_rows, bn),
                                         lambda j, i, layer=layer, off=off: (layer, k_block, off + j)))
    for ea, kind, off in extras:
        if kind == "tile":
            in_specs.append(pl.BlockSpec((bm, bn), lambda j, i, off=off: (i, off + j)))
        elif kind == "rows":
            in_specs.append(pl.BlockSpec((bm, ea.shape[1]), lambda j, i: (i, 0)))
        elif kind == "seg":
            in_specs.append(pl.BlockSpec((SEG_ROWS, bn), lambda j, i, off=off: (0, off + j)))
        elif kind == "whole":
            in_specs.append(pl.BlockSpec(ea.shape, lambda j, i: (0, 0)))
        else:
            assert kind == "col"
            in_specs.append(pl.BlockSpec((1, bn), lambda j, i, off=off: (0, off + j)))
    out_specs = [pl.BlockSpec((bm, bn), lambda j, i: (i, j)) for _ in outs]
    out_shape = [jax.ShapeDtypeStruct((m_rows, n_tiles * bn), dt) for dt in outs]
    scratch = []
    if stream:
        scratch = ([pltpu.VMEM((2, k_rows, bn), BF16) for _ in ws]
                   + [pltpu.VMEM((STAGE_SLOTS, k_rows // n_m, bn), F32) for _ in ws]
                   + [pltpu.SemaphoreType.DMA((len(ws), STAGE_SLOTS))])
    w_meta = tuple((xi, layer, k_block * k_rows, off) for _, layer, off, xi in ws)
    body = functools.partial(
        _fused_matmul_kernel, n_x=len(xs), w_meta=w_meta, e_kinds=tuple(e[1] for e in extras),
        n_o=len(outs), bm=bm, bn=bn, n_m=n_m, n_tiles=n_tiles, k_rows=k_rows, row_split=row_split,
        epilogue=epilogue, x_prologue=x_prologue, stream=stream)
    res = pl.pallas_call(
        body, grid=(n_tiles, n_m), in_specs=in_specs, out_specs=out_specs, out_shape=out_shape,
        scratch_shapes=scratch, compiler_params=_params(2), name=name,
    )(*xs, *[w[0] for w in ws], *[e[0] for e in extras])
    return res


def _row_block(m_rows):
    return m_rows // 8


BN_WIDE = 1024
BN_PAIR = 512
BN_F32_EXTRAS = 512


def _ep_bias(accs, extras, rows):
    return [accs[0] + extras[0]]


def _ep_swiglu(accs, extras, rows):
    a, b = accs
    return [a * _sigmoid(a) * b]


def _ep_residual(accs, extras, rows, *, weight):
    resid, gate = extras[:2]
    acc = accs[0] if len(extras) == 2 else extras[2] + accs[0]
    return [resid + (weight * _seg_select(gate, rows)) * acc]


def _ep_plain(accs, extras, rows):
    return [accs[0]]


def _ep_head_norm_rope(accs, extras, rows, *, scale):
    g, cosf, sinf = extras
    acc = accs[0]
    chunks = []
    for h in range(acc.shape[1] // HEAD_DIM):
        t = acc[:, h * HEAD_DIM:(h + 1) * HEAD_DIM]
        t = t * lax.rsqrt(jnp.mean(t * t, axis=-1, keepdims=True) + EPS) * g
        t = t * cosf + pltpu.roll(t, HEAD_DIM // 2, 1) * sinf
        chunks.append(t * scale if scale != 1.0 else t)
    return [jnp.concatenate(chunks, axis=1)]


def _ep_glu(accs, extras, rows):
    a, gt = accs
    return [a * _sigmoid(gt)]


def _ep_sigmoid(accs, extras, rows):
    return [_sigmoid(accs[0])]


def _ep_merge(accs, extras, rows):
    g_attn, g_conv = extras
    return [g_attn.astype(F32) * accs[0] + g_conv.astype(F32) * accs[1]]


def _silu(x):
    return x * _sigmoid(x)


NORM_ROWS = 256


def _norm_mod_kernel(h_ref, g_ref, shift_ref, scale_ref, o_ref):
    seg = pl.ds(jnp.minimum(pl.program_id(0) // (SEQ // NORM_ROWS), BATCH), 1)
    x = h_ref[...]
    inv = lax.rsqrt(jnp.mean(x * x, axis=-1, keepdims=True) + EPS)
    o_ref[...] = (x * inv * (g_ref[...] * (1.0 + scale_ref[seg, :])) + shift_ref[seg, :]).astype(BF16)


def _norm_mod(name, h, g, m_all, o, m_rows):
    return pl.pallas_call(
        _norm_mod_kernel, grid=(m_rows // NORM_ROWS,),
        in_specs=[pl.BlockSpec((NORM_ROWS, D_MODEL), lambda i: (i, 0)),
                  pl.BlockSpec((1, D_MODEL), lambda i: (0, 0)),
                  pl.BlockSpec((SEG_ROWS, D_MODEL), lambda i: (0, o)),
                  pl.BlockSpec((SEG_ROWS, D_MODEL), lambda i: (0, o + 1))],
        out_specs=pl.BlockSpec((NORM_ROWS, D_MODEL), lambda i: (i, 0)),
        out_shape=jax.ShapeDtypeStruct((m_rows, D_MODEL), BF16),
        compiler_params=_params(1), name=name,
    )(h, g.reshape(1, D_MODEL), m_all, m_all)


ATTN_Q_ROWS = 256
ATTN_KV_ROWS = 512
LAT_Q_BLOCKS = SEQ // ATTN_Q_ROWS


def _attn_kernel(q_ref, kl_ref, vl_ref, kc_ref, vc_ref, o_ref, m_sc, acc_sc, s0_sc, s1_sc):
    qi = pl.program_id(2)
    m_sc[...] = jnp.full(m_sc.shape, -jnp.inf, F32)
    acc_sc[...] = jnp.zeros(acc_sc.shape, F32)

    def scores(k):
        return tuple(
            lax.dot_general(q_ref[:, g * HEAD_DIM:(g + 1) * HEAD_DIM], k, (((1,), (1,)), ((), ())),
                            preferred_element_type=F32)
            for g in range(GROUP))

    def update(scores_kv, v):
        v_ones = jnp.concatenate([v, jnp.ones(v.shape, BF16)], axis=1)
        for g, s in enumerate(scores_kv):
            chunks = [s[:, c * LANES:(c + 1) * LANES] for c in range(s.shape[1] // LANES)]
            m_prev = m_sc[g]
            m_new = jnp.maximum(m_prev, jnp.max(functools.reduce(jnp.maximum, chunks), axis=-1, keepdims=True))
            alpha = jnp.exp(m_prev - m_new)
            p = jnp.concatenate([jnp.exp(c - m_new).astype(BF16) for c in chunks], axis=1)
            pv = jnp.dot(p, v_ones, preferred_element_type=F32)
            acc_sc[g] = jnp.concatenate([alpha, alpha], axis=1) * acc_sc[g] + pv
            m_sc[g] = m_new

    def lat_rows(t):
        return pl.ds(pl.multiple_of(t * ATTN_KV_ROWS, ATTN_KV_ROWS), ATTN_KV_ROWS)

    @pl.when(qi < LAT_Q_BLOCKS)
    def _():
        def put(sc, s_kv):
            for g, s in enumerate(s_kv):
                sc[g] = s

        def get(sc):
            return tuple(sc[g] for g in range(GROUP))

        def body(tt, carry):
            t = 2 * tt
            put(s1_sc, scores(kl_ref[lat_rows(t + 1), :]))
            update(get(s0_sc), vl_ref[lat_rows(t), :])
            put(s0_sc, scores(kl_ref[lat_rows(t + 2), :]))
            update(get(s1_sc), vl_ref[lat_rows(t + 1), :])
            return carry
        n_lat = SEQ // ATTN_KV_ROWS
        put(s0_sc, scores(kl_ref[lat_rows(0), :]))
        lax.fori_loop(0, n_lat // 2 - 1, body, 0)
        put(s1_sc, scores(kl_ref[lat_rows(n_lat - 1), :]))
        update(get(s0_sc), vl_ref[lat_rows(n_lat - 2), :])
        update(get(s1_sc), vl_ref[lat_rows(n_lat - 1), :])

    update(scores(kc_ref[...]), vc_ref[...])
    for g in range(GROUP):
        acc = acc_sc[g]
        o_ref[:, g * HEAD_DIM:(g + 1) * HEAD_DIM] = (acc[:, :HEAD_DIM] / acc[:, HEAD_DIM:]).astype(BF16)


def _attention(name, q, k, v, with_ctx_queries):
    ctx_q_blocks = CTX_LEN // ATTN_Q_ROWS
    n_q = LAT_Q_BLOCKS + (ctx_q_blocks if with_ctx_queries else 0)
    m_out = M_ALL if with_ctx_queries else M_LAT
    ctx_block0 = M_LAT // CTX_LEN

    def q_map(b, h, qi):
        ctx_q = M_LAT // ATTN_Q_ROWS + b * ctx_q_blocks + (qi - LAT_Q_BLOCKS)
        return (jnp.where(qi < LAT_Q_BLOCKS, b * LAT_Q_BLOCKS + qi, ctx_q), h)

    lat_spec = pl.BlockSpec((SEQ, HEAD_DIM), lambda b, h, qi: (b, h))
    ctx_spec = pl.BlockSpec((CTX_LEN, HEAD_DIM), lambda b, h, qi: (ctx_block0 + b, h))
    stats = pltpu.VMEM((GROUP, ATTN_Q_ROWS, HEAD_DIM), F32)
    return pl.pallas_call(
        _attn_kernel, grid=(BATCH, N_KV_HEADS, n_q),
        in_specs=[pl.BlockSpec((ATTN_Q_ROWS, GROUP * HEAD_DIM), q_map), lat_spec, lat_spec, ctx_spec, ctx_spec],
        out_specs=pl.BlockSpec((ATTN_Q_ROWS, GROUP * HEAD_DIM), q_map),
        out_shape=jax.ShapeDtypeStruct((m_out, ATTN_WIDTH), BF16),
        scratch_shapes=[stats, pltpu.VMEM((GROUP, ATTN_Q_ROWS, 2 * HEAD_DIM), F32),
                        pltpu.VMEM((GROUP, ATTN_Q_ROWS, ATTN_KV_ROWS), F32),
                        pltpu.VMEM((GROUP, ATTN_Q_ROWS, ATTN_KV_ROWS), F32)],
        compiler_params=_params(3), name=name,
    )(q, k, v, k, v)


CONV_ROWS = 256
HALO = 16
LANES = 128
SUBLANES = 8


def _conv_kernel(prev_ref, cur_ref, next_ref, w_ref, b_ref, g_ref, o_ref, pad_sc, y_sc):
    i = pl.program_id(0)
    lat_blocks = SEQ // CONV_ROWS
    n_lat = BATCH * lat_blocks
    first = (i % lat_blocks == 0) | (i >= n_lat)
    last = (i % lat_blocks == lat_blocks - 1) | (i >= n_lat)
    pad_sc[0:HALO, :] = jnp.where(first, 0.0, prev_ref[...].astype(F32))
    pad_sc[HALO:HALO + CONV_ROWS, :] = cur_ref[...].astype(F32)
    pad_sc[HALO + CONV_ROWS:, :] = jnp.where(last, 0.0, next_ref[...].astype(F32))

    def lanes(c):
        return pl.ds(pl.multiple_of(c * LANES, LANES), LANES)

    def chunk(c, ss):
        col = lanes(c)
        acc = jnp.broadcast_to(b_ref[:, col], (CONV_ROWS, LANES))
        for r in range(SUBLANES):
            part = None
            for k in range(CONV_WIDTH):
                start = HALO - CONV_PAD + k
                if start % SUBLANES != r:
                    continue
                base = start - r
                term = pad_sc[base:base + CONV_ROWS + SUBLANES, col] * w_ref[k:k + 1, col]
                part = term if part is None else part + term
            acc = acc + part[r:r + CONV_ROWS, :]
        y_sc[:, col] = acc
        return ss + acc * acc

    ss = lax.fori_loop(0, CONV_CH // LANES, chunk, jnp.zeros((CONV_ROWS, LANES), F32))
    inv = jnp.broadcast_to(lax.rsqrt(jnp.sum(ss, axis=-1, keepdims=True) * (1.0 / CONV_CH) + EPS),
                           (CONV_ROWS, LANES))

    def finish(c, carry):
        col = lanes(c)
        o_ref[:, col] = _silu(y_sc[:, col] * inv * g_ref[:, col]).astype(BF16)
        return carry

    lax.fori_loop(0, CONV_CH // LANES, finish, 0)


def _conv_module(name, u, w_dw, b_dw, g_norm, m_rows):
    assert CTX_LEN == CONV_ROWS
    per = CONV_ROWS // HALO
    n_halo = u.shape[0] // HALO
    return pl.pallas_call(
        _conv_kernel, grid=(m_rows // CONV_ROWS,),
        in_specs=[pl.BlockSpec((HALO, CONV_CH), lambda i: (jnp.maximum(i * per - 1, 0), 0)),
                  pl.BlockSpec((CONV_ROWS, CONV_CH), lambda i: (i, 0)),
                  pl.BlockSpec((HALO, CONV_CH), lambda i: (jnp.minimum((i + 1) * per, n_halo - 1), 0)),
                  pl.BlockSpec((CONV_WIDTH, CONV_CH), lambda i: (0, 0)),
                  pl.BlockSpec((1, CONV_CH), lambda i: (0, 0)),
                  pl.BlockSpec((1, CONV_CH), lambda i: (0, 0))],
        out_specs=pl.BlockSpec((CONV_ROWS, CONV_CH), lambda i: (i, 0)),
        out_shape=jax.ShapeDtypeStruct((m_rows, CONV_CH), BF16),
        scratch_shapes=[pltpu.VMEM((CONV_ROWS + 2 * HALO, CONV_CH), F32), pltpu.VMEM((CONV_ROWS, CONV_CH), F32)],
        compiler_params=_params(1), name=name,
    )(u, u, u, w_dw, b_dw.reshape(1, CONV_CH), g_norm.reshape(1, CONV_CH))


def _rope_tables():
    rows = SEQ // GRID_W
    row_idx = jnp.repeat(jnp.arange(rows, dtype=F32), GRID_W)
    col_idx = jnp.tile(jnp.arange(GRID_W, dtype=F32), rows)
    axis_dim = HEAD_DIM // 2
    inv_freq = ROPE_THETA ** (-jnp.arange(0, axis_dim, 2, dtype=F32) / axis_dim)
    ang = jnp.concatenate([row_idx[:, None] * inv_freq, col_idx[:, None] * inv_freq], axis=-1)
    cos, sin = jnp.cos(ang), jnp.sin(ang)
    cosf = jnp.concatenate([cos, cos], axis=-1)
    sinf = jnp.concatenate([-sin, sin], axis=-1)
    cosf = jnp.concatenate([cosf, cosf, jnp.ones((M_CTX, HEAD_DIM), F32)], axis=0)
    sinf = jnp.concatenate([sinf, sinf, jnp.zeros((M_CTX, HEAD_DIM), F32)], axis=0)
    return cosf, sinf


def _half_ffn(tag, h, m_all, g, w_in, w_out, layer, o, m_rows):
    hn = _norm_mod(f"{tag}_norm", h, g, m_all, o, m_rows)
    bm = _row_block(m_rows)
    bn = BN_PAIR
    (u,) = _fused_matmul(f"{tag}_in", [hn], [(w_in, layer, 0, 0), (w_in, layer, D_FF // bn, 0)], [],
                         _ep_swiglu, [BF16], m_rows=m_rows, bm=bm, bn=bn, n_tiles=D_FF // bn)
    half = D_FF // 2
    bn = BN_F32_EXTRAS
    (part,) = _fused_matmul(f"{tag}_out_a", [u], [(w_out, layer, 0, 0)], [], _ep_plain, [F32],
                            m_rows=m_rows, bm=bm, bn=bn, n_tiles=D_MODEL // bn, k_width=half, k_block=0)
    bn = BN_F32_EXTRAS
    (h_new,) = _fused_matmul(
        f"{tag}_out_b", [u], [(w_out, layer, 0, 0)],
        [(h, "tile", 0), (m_all, "seg", (o + 2) * D_MODEL // bn), (part, "tile", 0)],
        functools.partial(_ep_residual, weight=0.5), [F32],
        m_rows=m_rows, bm=bm, bn=bn, n_tiles=D_MODEL // bn, k_width=half, k_block=1)
    return h_new


def kernel(x, c, ctx, c_ctx, w_mod, b_mod, norm_ffn1, w_ffn1_in, w_ffn1_out, norm_mix, w_in, q_norm, k_norm,
           w_attn_o, conv_dw, conv_b, conv_norm, w_conv_o, w_out, norm_ffn2, w_ffn2_in, w_ffn2_out):
    h = jnp.concatenate([x.reshape(M_LAT, D_MODEL), ctx.reshape(M_CTX, D_MODEL)], axis=0)
    cvec = jnp.concatenate([c, c_ctx[None, :], jnp.zeros((SEG_ROWS - BATCH - 1, D_MODEL), F32)], axis=0)
    cosf, sinf = _rope_tables()
    bn = BN_F32_EXTRAS

    for layer in range(DEPTH):
        tag = f"l{layer}"
        update_ctx = layer < DEPTH - 1
        (m_all,) = _fused_matmul(
            f"{tag}_mod", [cvec], [(w_mod, layer, 0, 0)], [(b_mod[layer].reshape(1, -1), "col", 0)],
            _ep_bias, [F32], m_rows=SEG_ROWS, bm=SEG_ROWS, bn=bn, n_tiles=N_MOD * D_MODEL // bn, x_prologue=_silu,
            row_split=1)

        h = _half_ffn(f"{tag}_ffn1", h, m_all, norm_ffn1[layer], w_ffn1_in, w_ffn1_out, layer, 0, M_ALL)

        hn = _norm_mod(f"{tag}_mix_norm", h, norm_mix[layer], m_all, 3, M_ALL)
        m_mix = M_ALL if update_ctx else M_LAT
        rope = [(cosf, "rows", 0), (sinf, "rows", 0)]
        (k,) = _fused_matmul(
            f"{tag}_k", [hn], [(w_in, layer, 0, 0)], [(k_norm[layer].reshape(1, HEAD_DIM), "whole", 0)] + rope,
            functools.partial(_ep_head_norm_rope, scale=1.0), [BF16],
            m_rows=M_ALL, bm=_row_block(M_ALL), bn=bn, n_tiles=KV_WIDTH // bn, row_split=4)
        (v,) = _fused_matmul(
            f"{tag}_v", [hn], [(w_in, layer, KV_WIDTH // bn, 0)], [], _ep_plain, [BF16],
            m_rows=M_ALL, bm=_row_block(M_ALL), bn=bn, n_tiles=KV_WIDTH // bn)
        bm = _row_block(m_mix)
        bw = BN_WIDE
        (q,) = _fused_matmul(
            f"{tag}_q", [hn], [(w_in, layer, KV_COLS // bn, 0)],
            [(q_norm[layer].reshape(1, HEAD_DIM), "whole", 0)] + rope,
            functools.partial(_ep_head_norm_rope, scale=HEAD_DIM ** -0.5), [BF16],
            m_rows=m_mix, bm=bm, bn=bn, n_tiles=ATTN_WIDTH // bn, row_split=4)
        bg = BN_PAIR
        (u,) = _fused_matmul(
            f"{tag}_glu", [hn], [(w_in, layer, Q_END // bg, 0), (w_in, layer, (Q_END + CONV_CH) // bg, 0)], [],
            _ep_glu, [BF16], m_rows=m_mix, bm=bm, bn=bg, n_tiles=CONV_CH // bg)
        (gates,) = _fused_matmul(
            f"{tag}_gates", [hn], [(w_in, layer, GLU_END // bw, 0)], [], _ep_sigmoid, [BF16],
            m_rows=m_mix, bm=bm, bn=bw, n_tiles=2 * D_MODEL // bw)

        attn = _attention(f"{tag}_attn", q, k, v, update_ctx)
        cv = _conv_module(f"{tag}_conv", u, conv_dw[layer], conv_b[layer], conv_norm[layer], m_mix)
        (merged,) = _fused_matmul(
            f"{tag}_merge", [attn, cv], [(w_attn_o, layer, 0, 0), (w_conv_o, layer, 0, 1)],
            [(gates, "tile", 0), (gates, "tile", D_MODEL // bg)], _ep_merge, [BF16],
            m_rows=m_mix, bm=bm, bn=bg, n_tiles=D_MODEL // bg)
        (h,) = _fused_matmul(
            f"{tag}_out", [merged], [(w_out, layer, 0, 0)],
            [(h, "tile", 0), (m_all, "seg", 5 * D_MODEL // bn)],
            functools.partial(_ep_residual, weight=1.0), [F32],
            m_rows=m_mix, bm=bm, bn=bn, n_tiles=D_MODEL // bn)

        h = _half_ffn(f"{tag}_ffn2", h, m_all, norm_ffn2[layer], w_ffn2_in, w_ffn2_out, layer, 6, m_mix)

    return h.reshape(BATCH, SEQ, D_MODEL)
```

```python
import functools

import jax
import jax.numpy as jnp
from jax import lax
from jax.experimental import pallas as pl
from jax.experimental.pallas import tpu as pltpu

D_MODEL = 4096
BATCH = 2
SEQ = 4096
DEPTH = 2
GRID_W = 64
CTX_LEN = 256
HEAD_DIM = 128
N_Q_HEADS = 16
N_KV_HEADS = 4
GROUP = N_Q_HEADS // N_KV_HEADS
ATTN_WIDTH = N_Q_HEADS * HEAD_DIM
KV_WIDTH = N_KV_HEADS * HEAD_DIM
CONV_CH = D_MODEL // 2
CONV_WIDTH = 31
CONV_PAD = CONV_WIDTH // 2
D_FF = 2 * D_MODEL
N_MOD = 9
ROPE_THETA = 10000.0
EPS = 1e-6
KV_COLS = 2 * KV_WIDTH
Q_END = KV_COLS + ATTN_WIDTH
GLU_END = Q_END + 2 * CONV_CH

M_LAT = BATCH * SEQ
M_CTX = BATCH * CTX_LEN
M_ALL = M_LAT + M_CTX
SEG_ROWS = 8

VMEM_LIMIT_BYTES = 56 * 1024 * 1024

F32 = jnp.float32
BF16 = jnp.bfloat16


def _params(n_axes):
    return pltpu.CompilerParams(dimension_semantics=("arbitrary",) * n_axes,
                                vmem_limit_bytes=VMEM_LIMIT_BYTES)


def _seg_select(p, rows):
    return jnp.where(rows < SEQ, p[0:1, :], jnp.where(rows < M_LAT, p[1:2, :], p[2:3, :]))


def _sigmoid(x):
    return 1.0 / (1.0 + jnp.exp(-x))


CAST_ROWS = 512
STAGE_SLOTS = 3
PREFETCH_SLOT = 2


def _aligned(v, m):
    return v if isinstance(v, int) else pl.multiple_of(v, m)


def _fused_matmul_kernel(*refs, n_x, w_meta, e_kinds, n_o, bm, bn, n_m, n_tiles, k_rows, row_split, epilogue,
                         x_prologue, stream):
    n_w = len(w_meta)
    n_e = len(e_kinds)
    x_refs = refs[:n_x]
    w_refs = refs[n_x:n_x + n_w]
    e_refs = refs[n_x + n_w:n_x + n_w + n_e]
    o_refs = refs[n_x + n_w + n_e:n_x + n_w + n_e + n_o]
    scratch = refs[n_x + n_w + n_e + n_o:]
    j = pl.program_id(0)
    i = pl.program_id(1)

    if stream:
        wbf_refs, stage_refs, sems = scratch[:n_w], scratch[n_w:2 * n_w], scratch[2 * n_w]
        kc = k_rows // n_m

        def chunk_copy(w, tile, c, slot):
            _, layer, row0, off = w_meta[w]
            src = w_refs[w].at[layer, pl.ds(_aligned(row0 + c * kc, kc), kc),
                               pl.ds(_aligned((off + tile) * bn, bn), bn)]
            return pltpu.make_async_copy(src, stage_refs[w].at[slot], sems.at[w, slot])

        def cast_chunk(w, half, c, slot):
            wbf_refs[w][half, pl.ds(_aligned(c * kc, kc), kc), :] = stage_refs[w][slot].astype(BF16)

        prefetch = j + 1 < n_tiles

        @pl.when(prefetch)
        def _():
            for w in range(n_w):
                chunk_copy(w, j + 1, i, PREFETCH_SLOT).start()

        @pl.when((j == 0) & (i == 0))
        def _():
            for w in range(n_w):
                chunk_copy(w, 0, 0, 0).start()
            for c in range(n_m):
                for w in range(n_w):
                    if c + 1 < n_m:
                        chunk_copy(w, 0, c + 1, (c + 1) % 2).start()
                    chunk_copy(w, 0, c, c % 2).wait()
                    cast_chunk(w, 0, c, c % 2)

        cur = j % 2
        w_vals = [wbf_ref[cur] for wbf_ref in wbf_refs]
    elif n_m > 1:
        @pl.when(i == 0)
        def _():
            for w_ref, s_ref in zip(w_refs, scratch):
                def body(t, carry, w_ref=w_ref, s_ref=s_ref):
                    r = pl.ds(pl.multiple_of(t * CAST_ROWS, CAST_ROWS), CAST_ROWS)
                    s_ref[r, :] = w_ref[r, :].astype(BF16)
                    return carry
                lax.fori_loop(0, w_ref.shape[0] // CAST_ROWS, body, 0)
        w_vals = [s_ref[...] for s_ref in scratch]
    else:
        w_vals = [w_ref[...].astype(BF16) for w_ref in w_refs]
    w_x = [m[0] for m in w_meta]

    sub = bm // row_split
    for r in range(row_split):
        rs = slice(r * sub, (r + 1) * sub)
        xs = [x_ref[rs, :] for x_ref in x_refs]
        if x_prologue is not None:
            xs = [x_prologue(v).astype(BF16) for v in xs]
        accs = [jnp.dot(xs[xi], w, preferred_element_type=F32) for xi, w in zip(w_x, w_vals)]
        rows = i * bm + r * sub + lax.broadcasted_iota(jnp.int32, (sub, 1), 0)
        extras = [e_ref[rs, :] if kind in ("tile", "rows") else e_ref[...] for e_ref, kind in zip(e_refs, e_kinds)]
        outs = epilogue(accs, extras, rows)
        for o_ref, o in zip(o_refs, outs):
            o_ref[rs, :] = o.astype(o_ref.dtype)

    if stream:
        @pl.when(prefetch)
        def _():
            for w in range(n_w):
                chunk_copy(w, j + 1, i, PREFETCH_SLOT).wait()
                cast_chunk(w, 1 - cur, i, PREFETCH_SLOT)


def _fused_matmul(name, xs, ws, extras, epilogue, outs, *, m_rows, bm, bn, n_tiles, x_prologue=None,
                  k_width=None, k_block=0, row_split=4, stream=True):
    assert m_rows % bm == 0
    n_m = m_rows // bm
    stream = stream and n_m > 1
    k_of = lambda a: a.shape[-1 if a.ndim == 2 else 1] if k_width is None else k_width
    k_rows = k_of(ws[0][0])
    assert all(k_of(w[0]) == k_rows for w in ws) and all(k_of(xa) == k_rows for xa in xs)
    in_specs = [pl.BlockSpec((bm, k_rows), lambda j, i: (i, k_block)) for xa in xs]
    for wa, layer, off, _ in ws:
        if stream:
            assert k_rows % n_m == 0 and (k_rows // n_m) % 16 == 0
            in_specs.append(pl.BlockSpec(memory_space=pl.ANY))
        else:
            in_specs.append(pl.BlockSpec((None, k_rows, bn),
                                         lambda j, i, layer=layer, off=off: (layer, k_block, off + j)))
    for ea, kind, off in extras:
        if kind == "tile":
            in_specs.append(pl.BlockSpec((bm, bn), lambda j, i, off=off: (i, off + j)))
        elif kind == "rows":
            in_specs.append(pl.BlockSpec((bm, ea.shape[1]), lambda j, i: (i, 0)))
        elif kind == "seg":
            in_specs.append(pl.BlockSpec((SEG_ROWS, bn), lambda j, i, off=off: (0, off + j)))
        elif kind == "whole":
            in_specs.append(pl.BlockSpec(ea.shape, lambda j, i: (0, 0)))
        else:
            assert kind == "col"
            in_specs.append(pl.BlockSpec((1, bn), lambda j, i, off=off: (0, off + j)))
    out_specs = [pl.BlockSpec((bm, bn), lambda j, i: (i, j)) for _ in outs]
    out_shape = [jax.ShapeDtypeStruct((m_rows, n_tiles * bn), dt) for dt in outs]
    scratch = []
    if stream:
        scratch = ([pltpu.VMEM((2, k_rows, bn), BF16) for _ in ws]
                   + [pltpu.VMEM((STAGE_SLOTS, k_rows // n_m, bn), F32) for _ in ws]
                   + [pltpu.SemaphoreType.DMA((len(ws), STAGE_SLOTS))])
    elif n_m > 1:
        assert k_rows % CAST_ROWS == 0
        scratch = [pltpu.VMEM((k_rows, bn), BF16) for _ in ws]
    w_meta = tuple((xi, layer, k_block * k_rows, off) for _, layer, off, xi in ws)
    body = functools.partial(
        _fused_matmul_kernel, n_x=len(xs), w_meta=w_meta, e_kinds=tuple(e[1] for e in extras),
        n_o=len(outs), bm=bm, bn=bn, n_m=n_m, n_tiles=n_tiles, k_rows=k_rows, row_split=row_split,
        epilogue=epilogue, x_prologue=x_prologue, stream=stream)
    res = pl.pallas_call(
        body, grid=(n_tiles, n_m), in_specs=in_specs, out_specs=out_specs, out_shape=out_shape,
        scratch_shapes=scratch, compiler_params=_params(2), name=name,
    )(*xs, *[w[0] for w in ws], *[e[0] for e in extras])
    return res


def _row_block(m_rows):
    return m_rows // 8


BN_WIDE = 1024
BN_PAIR = 512
BN_F32_EXTRAS = 512


def _ep_bias(accs, extras, rows):
    return [accs[0] + extras[0]]


def _ep_swiglu(accs, extras, rows):
    a, b = accs
    return [a * _sigmoid(a) * b]


def _ep_residual(accs, extras, rows, *, weight):
    resid, gate = extras[:2]
    acc = accs[0] if len(extras) == 2 else extras[2] + accs[0]
    return [resid + (weight * _seg_select(gate, rows)) * acc]


def _ep_plain(accs, extras, rows):
    return [accs[0]]


def _ep_head_norm_rope(accs, extras, rows, *, scale):
    g, cosf, sinf = extras
    acc = accs[0]
    chunks = []
    for h in range(acc.shape[1] // HEAD_DIM):
        t = acc[:, h * HEAD_DIM:(h + 1) * HEAD_DIM]
        t = t * lax.rsqrt(jnp.mean(t * t, axis=-1, keepdims=True) + EPS) * g
        t = t * cosf + pltpu.roll(t, HEAD_DIM // 2, 1) * sinf
        chunks.append(t * scale if scale != 1.0 else t)
    return [jnp.concatenate(chunks, axis=1)]


def _ep_glu(accs, extras, rows):
    a, gt = accs
    return [a * _sigmoid(gt)]


def _ep_sigmoid(accs, extras, rows):
    return [_sigmoid(accs[0])]


def _ep_merge(accs, extras, rows):
    g_attn, g_conv = extras
    return [g_attn.astype(F32) * accs[0] + g_conv.astype(F32) * accs[1]]


def _silu(x):
    return x * _sigmoid(x)


NORM_ROWS = 256


def _norm_mod_kernel(h_ref, g_ref, shift_ref, scale_ref, o_ref):
    seg = pl.ds(jnp.minimum(pl.program_id(0) // (SEQ // NORM_ROWS), BATCH), 1)
    x = h_ref[...]
    inv = lax.rsqrt(jnp.mean(x * x, axis=-1, keepdims=True) + EPS)
    o_ref[...] = (x * inv * (g_ref[...] * (1.0 + scale_ref[seg, :])) + shift_ref[seg, :]).astype(BF16)


def _norm_mod(name, h, g, m_all, o, m_rows):
    return pl.pallas_call(
        _norm_mod_kernel, grid=(m_rows // NORM_ROWS,),
        in_specs=[pl.BlockSpec((NORM_ROWS, D_MODEL), lambda i: (i, 0)),
                  pl.BlockSpec((1, D_MODEL), lambda i: (0, 0)),
                  pl.BlockSpec((SEG_ROWS, D_MODEL), lambda i: (0, o)),
                  pl.BlockSpec((SEG_ROWS, D_MODEL), lambda i: (0, o + 1))],
        out_specs=pl.BlockSpec((NORM_ROWS, D_MODEL), lambda i: (i, 0)),
        out_shape=jax.ShapeDtypeStruct((m_rows, D_MODEL), BF16),
        compiler_params=_params(1), name=name,
    )(h, g.reshape(1, D_MODEL), m_all, m_all)


ATTN_Q_ROWS = 256
ATTN_KV_ROWS = 512
LAT_Q_BLOCKS = SEQ // ATTN_Q_ROWS


def _attn_kernel(q_ref, kl_ref, vl_ref, kc_ref, vc_ref, o_ref, m_sc, acc_sc, s0_sc, s1_sc):
    qi = pl.program_id(2)
    m_sc[...] = jnp.full(m_sc.shape, -jnp.inf, F32)
    acc_sc[...] = jnp.zeros(acc_sc.shape, F32)

    def scores(k):
        return tuple(
            lax.dot_general(q_ref[:, g * HEAD_DIM:(g + 1) * HEAD_DIM], k, (((1,), (1,)), ((), ())),
                            preferred_element_type=F32)
            for g in range(GROUP))

    def update(scores_kv, v):
        v_ones = jnp.concatenate([v, jnp.ones(v.shape, BF16)], axis=1)
        for g, s in enumerate(scores_kv):
            chunks = [s[:, c * LANES:(c + 1) * LANES] for c in range(s.shape[1] // LANES)]
            m_prev = m_sc[g]
            m_new = jnp.maximum(m_prev, jnp.max(functools.reduce(jnp.maximum, chunks), axis=-1, keepdims=True))
            alpha = jnp.exp(m_prev - m_new)
            p = jnp.concatenate([jnp.exp(c - m_new).astype(BF16) for c in chunks], axis=1)
            pv = jnp.dot(p, v_ones, preferred_element_type=F32)
            acc_sc[g] = jnp.concatenate([alpha, alpha], axis=1) * acc_sc[g] + pv
            m_sc[g] = m_new

    def lat_rows(t):
        return pl.ds(pl.multiple_of(t * ATTN_KV_ROWS, ATTN_KV_ROWS), ATTN_KV_ROWS)

    @pl.when(qi < LAT_Q_BLOCKS)
    def _():
        def put(sc, s_kv):
            for g, s in enumerate(s_kv):
                sc[g] = s

        def get(sc):
            return tuple(sc[g] for g in range(GROUP))

        def body(tt, carry):
            t = 2 * tt
            put(s1_sc, scores(kl_ref[lat_rows(t + 1), :]))
            update(get(s0_sc), vl_ref[lat_rows(t), :])
            put(s0_sc, scores(kl_ref[lat_rows(t + 2), :]))
            update(get(s1_sc), vl_ref[lat_rows(t + 1), :])
            return carry
        n_lat = SEQ // ATTN_KV_ROWS
        put(s0_sc, scores(kl_ref[lat_rows(0), :]))
        lax.fori_loop(0, n_lat // 2 - 1, body, 0)
        put(s1_sc, scores(kl_ref[lat_rows(n_lat - 1), :]))
        update(get(s0_sc), vl_ref[lat_rows(n_lat - 2), :])
        update(get(s1_sc), vl_ref[lat_rows(n_lat - 1), :])

    update(scores(kc_ref[...]), vc_ref[...])
    for g in range(GROUP):
        acc = acc_sc[g]
        o_ref[:, g * HEAD_DIM:(g + 1) * HEAD_DIM] = (acc[:, :HEAD_DIM] / acc[:, HEAD_DIM:]).astype(BF16)


def _attention(name, q, k, v, with_ctx_queries):
    ctx_q_blocks = CTX_LEN // ATTN_Q_ROWS
    n_q = LAT_Q_BLOCKS + (ctx_q_blocks if with_ctx_queries else 0)
    m_out = M_ALL if with_ctx_queries else M_LAT
    ctx_block0 = M_LAT // CTX_LEN

    def q_map(b, h, qi):
        ctx_q = M_LAT // ATTN_Q_ROWS + b * ctx_q_blocks + (qi - LAT_Q_BLOCKS)
        return (jnp.where(qi < LAT_Q_BLOCKS, b * LAT_Q_BLOCKS + qi, ctx_q), h)

    lat_spec = pl.BlockSpec((SEQ, HEAD_DIM), lambda b, h, qi: (b, h))
    ctx_spec = pl.BlockSpec((CTX_LEN, HEAD_DIM), lambda b, h, qi: (ctx_block0 + b, h))
    stats = pltpu.VMEM((GROUP, ATTN_Q_ROWS, HEAD_DIM), F32)
    return pl.pallas_call(
        _attn_kernel, grid=(BATCH, N_KV_HEADS, n_q),
        in_specs=[pl.BlockSpec((ATTN_Q_ROWS, GROUP * HEAD_DIM), q_map), lat_spec, lat_spec, ctx_spec, ctx_spec],
        out_specs=pl.BlockSpec((ATTN_Q_ROWS, GROUP * HEAD_DIM), q_map),
        out_shape=jax.ShapeDtypeStruct((m_out, ATTN_WIDTH), BF16),
        scratch_shapes=[stats, pltpu.VMEM((GROUP, ATTN_Q_ROWS, 2 * HEAD_DIM), F32),
                        pltpu.VMEM((GROUP, ATTN_Q_ROWS, ATTN_KV_ROWS), F32),
                        pltpu.VMEM((GROUP, ATTN_Q_ROWS, ATTN_KV_ROWS), F32)],
        compiler_params=_params(3), name=name,
    )(q, k, v, k, v)


CONV_ROWS = 256
HALO = 16
LANES = 128
SUBLANES = 8


def _conv_kernel(prev_ref, cur_ref, next_ref, w_ref, b_ref, g_ref, o_ref, pad_sc, y_sc):
    i = pl.program_id(0)
    lat_blocks = SEQ // CONV_ROWS
    n_lat = BATCH * lat_blocks
    first = (i % lat_blocks == 0) | (i >= n_lat)
    last = (i % lat_blocks == lat_blocks - 1) | (i >= n_lat)
    pad_sc[0:HALO, :] = jnp.where(first, 0.0, prev_ref[...].astype(F32))
    pad_sc[HALO:HALO + CONV_ROWS, :] = cur_ref[...].astype(F32)
    pad_sc[HALO + CONV_ROWS:, :] = jnp.where(last, 0.0, next_ref[...].astype(F32))

    def lanes(c):
        return pl.ds(pl.multiple_of(c * LANES, LANES), LANES)

    def chunk(c, ss):
        col = lanes(c)
        acc = jnp.broadcast_to(b_ref[:, col], (CONV_ROWS, LANES))
        for r in range(SUBLANES):
            part = None
            for k in range(CONV_WIDTH):
                start = HALO - CONV_PAD + k
                if start % SUBLANES != r:
                    continue
                base = start - r
                term = pad_sc[base:base + CONV_ROWS + SUBLANES, col] * w_ref[k:k + 1, col]
                part = term if part is None else part + term
            acc = acc + part[r:r + CONV_ROWS, :]
        y_sc[:, col] = acc
        return ss + acc * acc

    ss = lax.fori_loop(0, CONV_CH // LANES, chunk, jnp.zeros((CONV_ROWS, LANES), F32))
    inv = jnp.broadcast_to(lax.rsqrt(jnp.sum(ss, axis=-1, keepdims=True) * (1.0 / CONV_CH) + EPS),
                           (CONV_ROWS, LANES))

    def finish(c, carry):
        col = lanes(c)
        o_ref[:, col] = _silu(y_sc[:, col] * inv * g_ref[:, col]).astype(BF16)
        return carry

    lax.fori_loop(0, CONV_CH // LANES, finish, 0)


def _conv_module(name, u, w_dw, b_dw, g_norm, m_rows):
    assert CTX_LEN == CONV_ROWS
    per = CONV_ROWS // HALO
    n_halo = u.shape[0] // HALO
    return pl.pallas_call(
        _conv_kernel, grid=(m_rows // CONV_ROWS,),
        in_specs=[pl.BlockSpec((HALO, CONV_CH), lambda i: (jnp.maximum(i * per - 1, 0), 0)),
                  pl.BlockSpec((CONV_ROWS, CONV_CH), lambda i: (i, 0)),
                  pl.BlockSpec((HALO, CONV_CH), lambda i: (jnp.minimum((i + 1) * per, n_halo - 1), 0)),
                  pl.BlockSpec((CONV_WIDTH, CONV_CH), lambda i: (0, 0)),
                  pl.BlockSpec((1, CONV_CH), lambda i: (0, 0)),
                  pl.BlockSpec((1, CONV_CH), lambda i: (0, 0))],
        out_specs=pl.BlockSpec((CONV_ROWS, CONV_CH), lambda i: (i, 0)),
        out_shape=jax.ShapeDtypeStruct((m_rows, CONV_CH), BF16),
        scratch_shapes=[pltpu.VMEM((CONV_ROWS + 2 * HALO, CONV_CH), F32), pltpu.VMEM((CONV_ROWS, CONV_CH), F32)],
        compiler_params=_params(1), name=name,
    )(u, u, u, w_dw, b_dw.reshape(1, CONV_CH), g_norm.reshape(1, CONV_CH))


def _rope_tables():
    rows = SEQ // GRID_W
    row_idx = jnp.repeat(jnp.arange(rows, dtype=F32), GRID_W)
    col_idx = jnp.tile(jnp.arange(GRID_W, dtype=F32), rows)
    axis_dim = HEAD_DIM // 2
    inv_freq = ROPE_THETA ** (-jnp.arange(0, axis_dim, 2, dtype=F32) / axis_dim)
    ang = jnp.concatenate([row_idx[:, None] * inv_freq, col_idx[:, None] * inv_freq], axis=-1)
    cos, sin = jnp.cos(ang), jnp.sin(ang)
    cosf = jnp.concatenate([cos, cos], axis=-1)
    sinf = jnp.concatenate([-sin, sin], axis=-1)
    cosf = jnp.concatenate([cosf, cosf, jnp.ones((M_CTX, HEAD_DIM), F32)], axis=0)
    sinf = jnp.concatenate([sinf, sinf, jnp.zeros((M_CTX, HEAD_DIM), F32)], axis=0)
    return cosf, sinf


def _half_ffn(tag, h, m_all, g, w_in, w_out, layer, o, m_rows):
    hn = _norm_mod(f"{tag}_norm", h, g, m_all, o, m_rows)
    bm = _row_block(m_rows)
    bn = BN_PAIR
    (u,) = _fused_matmul(f"{tag}_in", [hn], [(w_in, layer, 0, 0), (w_in, layer, D_FF // bn, 0)], [],
                         _ep_swiglu, [BF16], m_rows=m_rows, bm=bm, bn=bn, n_tiles=D_FF // bn)
    half = D_FF // 2
    bn = BN_F32_EXTRAS
    (part,) = _fused_matmul(f"{tag}_out_a", [u], [(w_out, layer, 0, 0)], [], _ep_plain, [F32],
                            m_rows=m_rows, bm=bm, bn=bn, n_tiles=D_MODEL // bn, k_width=half, k_block=0)
    bn = BN_F32_EXTRAS
    (h_new,) = _fused_matmul(
        f"{tag}_out_b", [u], [(w_out, layer, 0, 0)],
        [(h, "tile", 0), (m_all, "seg", (o + 2) * D_MODEL // bn), (part, "tile", 0)],
        functools.partial(_ep_residual, weight=0.5), [F32],
        m_rows=m_rows, bm=bm, bn=bn, n_tiles=D_MODEL // bn, k_width=half, k_block=1, stream=False)
    return h_new


def kernel(x, c, ctx, c_ctx, w_mod, b_mod, norm_ffn1, w_ffn1_in, w_ffn1_out, norm_mix, w_in, q_norm, k_norm,
           w_attn_o, conv_dw, conv_b, conv_norm, w_conv_o, w_out, norm_ffn2, w_ffn2_in, w_ffn2_out):
    h = jnp.concatenate([x.reshape(M_LAT, D_MODEL), ctx.reshape(M_CTX, D_MODEL)], axis=0)
    cvec = jnp.concatenate([c, c_ctx[None, :], jnp.zeros((SEG_ROWS - BATCH - 1, D_MODEL), F32)], axis=0)
    cosf, sinf = _rope_tables()
    bn = BN_F32_EXTRAS

    for layer in range(DEPTH):
        tag = f"l{layer}"
        update_ctx = layer < DEPTH - 1
        (m_all,) = _fused_matmul(
            f"{tag}_mod", [cvec], [(w_mod, layer, 0, 0)], [(b_mod[layer].reshape(1, -1), "col", 0)],
            _ep_bias, [F32], m_rows=SEG_ROWS, bm=SEG_ROWS, bn=bn, n_tiles=N_MOD * D_MODEL // bn, x_prologue=_silu,
            row_split=1)

        h = _half_ffn(f"{tag}_ffn1", h, m_all, norm_ffn1[layer], w_ffn1_in, w_ffn1_out, layer, 0, M_ALL)

        hn = _norm_mod(f"{tag}_mix_norm", h, norm_mix[layer], m_all, 3, M_ALL)
        m_mix = M_ALL if update_ctx else M_LAT
        rope = [(cosf, "rows", 0), (sinf, "rows", 0)]
        (k,) = _fused_matmul(
            f"{tag}_k", [hn], [(w_in, layer, 0, 0)], [(k_norm[layer].reshape(1, HEAD_DIM), "whole", 0)] + rope,
            functools.partial(_ep_head_norm_rope, scale=1.0), [BF16],
            m_rows=M_ALL, bm=_row_block(M_ALL), bn=bn, n_tiles=KV_WIDTH // bn, row_split=4, stream=False)
        (v,) = _fused_matmul(
            f"{tag}_v", [hn], [(w_in, layer, KV_WIDTH // bn, 0)], [], _ep_plain, [BF16],
            m_rows=M_ALL, bm=_row_block(M_ALL), bn=bn, n_tiles=KV_WIDTH // bn, stream=False)
        bm = _row_block(m_mix)
        bw = BN_WIDE
        (q,) = _fused_matmul(
            f"{tag}_q", [hn], [(w_in, layer, KV_COLS // bn, 0)],
            [(q_norm[layer].reshape(1, HEAD_DIM), "whole", 0)] + rope,
            functools.partial(_ep_head_norm_rope, scale=HEAD_DIM ** -0.5), [BF16],
            m_rows=m_mix, bm=bm, bn=bn, n_tiles=ATTN_WIDTH // bn, row_split=4)
        bg = BN_PAIR
        (u,) = _fused_matmul(
            f"{tag}_glu", [hn], [(w_in, layer, Q_END // bg, 0), (w_in, layer, (Q_END + CONV_CH) // bg, 0)], [],
            _ep_glu, [BF16], m_rows=m_mix, bm=bm, bn=bg, n_tiles=CONV_CH // bg)
        (gates,) = _fused_matmul(
            f"{tag}_gates", [hn], [(w_in, layer, GLU_END // bw, 0)], [], _ep_sigmoid, [BF16],
            m_rows=m_mix, bm=bm, bn=bw, n_tiles=2 * D_MODEL // bw)

        attn = _attention(f"{tag}_attn", q, k, v, update_ctx)
        cv = _conv_module(f"{tag}_conv", u, conv_dw[layer], conv_b[layer], conv_norm[layer], m_mix)
        (merged,) = _fused_matmul(
            f"{tag}_merge", [attn, cv], [(w_attn_o, layer, 0, 0), (w_conv_o, layer, 0, 1)],
            [(gates, "tile", 0), (gates, "tile", D_MODEL // bg)], _ep_merge, [BF16],
            m_rows=m_mix, bm=bm, bn=bg, n_tiles=D_MODEL // bg)
        (h,) = _fused_matmul(
            f"{tag}_out", [merged], [(w_out, layer, 0, 0)],
            [(h, "tile", 0), (m_all, "seg", 5 * D_MODEL // bn)],
            functools.partial(_ep_residual, weight=1.0), [F32],
            m_rows=m_mix, bm=bm, bn=bn, n_tiles=D_MODEL // bn, stream=False)

        h = _half_ffn(f"{tag}_ffn2", h, m_all, norm_ffn2[layer], w_ffn2_in, w_ffn2_out, layer, 6, m_mix)

    return h.reshape(BATCH, SEQ, D_MODEL)
```
